```python
import math
import jax, jax.numpy as jnp
from jax import lax
import numpy as np

D_MODEL = 1024
BATCH = 2
SEQ = 16384
DEPTH = 2

EPS = 1e-6
CHUNK = 128

RET_HEADS = 4
RET_DK = 256
RET_DV = 512
RET_QK = RET_HEADS * RET_DK
RET_V = RET_HEADS * RET_DV
ROPE_BASE = 10000.0

SSD_INNER = 2 * D_MODEL
SSD_HEADDIM = 64
SSD_HEADS = SSD_INNER // SSD_HEADDIM
SSD_GROUPS = 4
SSD_HPG = SSD_HEADS // SSD_GROUPS
SSD_STATE = 128
SSD_CONV = 4
SSD_BC = SSD_GROUPS * SSD_STATE
SSD_CONV_CH = SSD_INNER + 2 * SSD_BC

D_FF = -(-8 * D_MODEL // (3 * 256)) * 256

IN_SIZES = (RET_QK, RET_QK, RET_V, RET_V, SSD_INNER, SSD_CONV_CH, SSD_HEADS, D_MODEL, D_MODEL)
D_IN = sum(IN_SIZES)

kernel_name = "hybrid_retention_ssd_gated_block"


def rmsnorm(x, w):
    xf = x.astype(jnp.float32)
    y = xf * lax.rsqrt(jnp.mean(xf * xf, axis=-1, keepdims=True) + EPS)
    return (y * w.astype(jnp.float32)).astype(x.dtype)


def rotary(t, pos):
    half = t.shape[-1] // 2
    inv = ROPE_BASE ** (-jnp.arange(half, dtype=jnp.float32) / half)
    ang = pos.astype(jnp.float32)[:, None] * inv[None, :]
    cos = jnp.cos(ang)[None, :, None, :]
    sin = jnp.sin(ang)[None, :, None, :]
    t1, t2 = t[..., :half], t[..., half:]
    return jnp.concatenate([t1 * cos - t2 * sin, t1 * sin + t2 * cos], axis=-1)


def retention(q, k, v):
    bsz, s = q.shape[0], q.shape[1]
    nc = s // CHUNK
    log_g = jnp.log(1.0 - 2.0 ** (-5.0 - jnp.arange(RET_HEADS, dtype=jnp.float32)))
    idx = jnp.arange(CHUNK, dtype=jnp.float32)
    diff = idx[:, None] - idx[None, :]
    causal = diff >= 0
    intra = jnp.where(causal[None], jnp.exp(log_g[:, None, None] * jnp.maximum(diff, 0.0)[None]), 0.0)
    q_decay = jnp.exp(log_g[:, None] * (idx + 1.0))[None, :, :, None]
    k_decay = jnp.exp(log_g[:, None] * (CHUNK - 1.0 - idx))[None, :, :, None]
    chunk_decay = jnp.exp(log_g * CHUNK)[None, :, None, None]

    def to_chunks(t):
        return t.reshape(bsz, nc, CHUNK, RET_HEADS, t.shape[-1]).transpose(1, 0, 3, 2, 4)

    def step(state, inp):
        qc, kc, vc = inp
        sc = jnp.einsum('bhid,bhjd->bhij', qc, kc) * intra[None]
        y = jnp.einsum('bhij,bhjv->bhiv', sc, vc) + jnp.einsum('bhid,bhdv->bhiv', qc, state) * q_decay
        state = state * chunk_decay + jnp.einsum('bhjd,bhjv->bhdv', kc * k_decay, vc)
        return state, y

    init = jnp.zeros((bsz, RET_HEADS, RET_DK, RET_DV), jnp.float32)
    _, ys = lax.scan(step, init, (to_chunks(q), to_chunks(k), to_chunks(v)))
    return ys.transpose(1, 0, 3, 2, 4).reshape(bsz, s, RET_HEADS, RET_DV)


def ssd_scan(xdt, la, bm, cm):
    bsz, s = xdt.shape[0], xdt.shape[1]
    nc = s // CHUNK
    mask = jnp.tril(jnp.ones((CHUNK, CHUNK), dtype=bool))[None, :, :, None, None]

    def to_chunks(t):
        return jnp.moveaxis(t.reshape((bsz, nc, CHUNK) + t.shape[2:]), 1, 0)

    def step(state, inp):
        xc, lac, bc, cc = inp
        acs = jnp.cumsum(lac, axis=1)
        seg = acs[:, :, None] - acs[:, None, :]
        lmat = jnp.where(mask, jnp.exp(jnp.where(mask, seg, 0.0)), 0.0)
        cb = jnp.einsum('bign,bjgn->bijg', cc, bc)
        y = jnp.einsum('bijgr,bjgrp->bigrp', cb[..., None] * lmat, xc)
        y = y + jnp.einsum('bign,bgrpn->bigrp', cc, state) * jnp.exp(acs)[..., None]
        last = acs[:, -1]
        state = state * jnp.exp(last)[..., None, None] + jnp.einsum(
            'bjgn,bjgrp->bgrpn', bc, xc * jnp.exp(last[:, None] - acs)[..., None])
        return state, y

    init = jnp.zeros((bsz, SSD_GROUPS, SSD_HPG, SSD_HEADDIM, SSD_STATE), jnp.float32)
    _, ys = lax.scan(step, init, (to_chunks(xdt), to_chunks(la), to_chunks(bm), to_chunks(cm)))
    return jnp.moveaxis(ys, 0, 1).reshape(xdt.shape)


def causal_dwconv(x, w, b):
    out = lax.conv_general_dilated(
        x, w[:, None, :], window_strides=(1,), padding=[(SSD_CONV - 1, 0)],
        dimension_numbers=('NWC', 'WIO', 'NWC'), feature_group_count=x.shape[-1])
    return out + b


def hybrid_mixer(h, w_in, ret_out, conv_w, conv_b, dt_bias, a_log, d_skip, ssd_norm_w, ssd_out, w_o):
    bsz, s, _ = h.shape
    u = h @ w_in
    q, k, v, g_ret, z, xbc, dt, gate_ret, gate_ssd = jnp.split(u, list(np.cumsum(IN_SIZES)[:-1]), axis=-1)
    pos = jnp.arange(s)

    qf = rotary(q.astype(jnp.float32).reshape(bsz, s, RET_HEADS, RET_DK), pos)
    kf = rotary(k.astype(jnp.float32).reshape(bsz, s, RET_HEADS, RET_DK), pos) * (RET_DK ** -0.5)
    vf = v.astype(jnp.float32).reshape(bsz, s, RET_HEADS, RET_DV)
    yr = retention(qf, kf, vf)
    yr = yr * lax.rsqrt(jnp.mean(yr * yr, axis=-1, keepdims=True) + EPS)
    yr = (jax.nn.silu(g_ret.astype(jnp.float32)) * yr.reshape(bsz, s, RET_V)).astype(h.dtype)
    o_ret = yr @ ret_out

    xbc = jax.nn.silu(causal_dwconv(xbc, conv_w, conv_b))
    xs, bm, cm = jnp.split(xbc, [SSD_INNER, SSD_INNER + SSD_BC], axis=-1)
    xs = xs.astype(jnp.float32).reshape(bsz, s, SSD_GROUPS, SSD_HPG, SSD_HEADDIM)
    bm = bm.astype(jnp.float32).reshape(bsz, s, SSD_GROUPS, SSD_STATE)
    cm = cm.astype(jnp.float32).reshape(bsz, s, SSD_GROUPS, SSD_STATE)
    dtp = jax.nn.softplus(dt.astype(jnp.float32) + dt_bias.astype(jnp.float32))
    dtp = dtp.reshape(bsz, s, SSD_GROUPS, SSD_HPG)
    a = -jnp.exp(a_log.astype(jnp.float32)).reshape(SSD_GROUPS, SSD_HPG)
    ys = ssd_scan(xs * dtp[..., None], dtp * a, bm, cm)
    ys = ys + d_skip.astype(jnp.float32).reshape(SSD_GROUPS, SSD_HPG)[:, :, None] * xs
    ys = ys.reshape(bsz, s, SSD_INNER) * jax.nn.silu(z.astype(jnp.float32))
    ysg = ys.reshape(bsz, s, SSD_GROUPS, SSD_INNER // SSD_GROUPS)
    ysg = ysg * lax.rsqrt(jnp.mean(ysg * ysg, axis=-1, keepdims=True) + EPS)
    ys = (ysg.reshape(bsz, s, SSD_INNER) * ssd_norm_w.astype(jnp.float32)).astype(h.dtype)
    o_ssd = ys @ ssd_out

    merged = jax.nn.sigmoid(gate_ret) * o_ret + jax.nn.sigmoid(gate_ssd) * o_ssd
    return merged @ w_o


def swiglu(h, w_gate_up, w_down):
    gu = h @ w_gate_up
    gate, up = jnp.split(gu, [D_FF], axis=-1)
    return (jax.nn.silu(gate) * up) @ w_down


def setup_inputs(seed: int = 0) -> dict:
    key = jax.random.key(seed)
    ks = jax.random.split(key, 20)
    f32 = jnp.float32

    def nrm(k, shape, fan_in):
        return jax.random.normal(k, shape, f32) * (fan_in ** -0.5)

    def gain(k, shape):
        return 1.0 + 0.02 * jax.random.normal(k, shape, f32)

    dt0 = jnp.exp(jax.random.uniform(ks[6], (DEPTH, SSD_HEADS), f32, math.log(1e-3), math.log(1e-1)))
    dt_bias = dt0 + jnp.log(-jnp.expm1(-dt0))
    a_log = jnp.log(jax.random.uniform(ks[7], (DEPTH, SSD_HEADS), f32, 1.0, 16.0))
    return {
        "x": jax.random.normal(ks[0], (BATCH, SEQ, D_MODEL), f32),
        "norm_mix_w": gain(ks[1], (DEPTH, D_MODEL)),
        "w_in": nrm(ks[2], (DEPTH, D_MODEL, D_IN), D_MODEL),
        "ret_out": nrm(ks[3], (DEPTH, RET_V, D_MODEL), RET_V),
        "conv_w": nrm(ks[4], (DEPTH, SSD_CONV, SSD_CONV_CH), SSD_CONV),
        "conv_b": 0.02 * jax.random.normal(ks[5], (DEPTH, SSD_CONV_CH), f32),
        "dt_bias": dt_bias,
        "a_log": a_log,
        "d_skip": gain(ks[8], (DEPTH, SSD_HEADS)),
        "ssd_norm_w": gain(ks[9], (DEPTH, SSD_INNER)),
        "ssd_out": nrm(ks[10], (DEPTH, SSD_INNER, D_MODEL), SSD_INNER),
        "w_o": nrm(ks[11], (DEPTH, D_MODEL, D_MODEL), D_MODEL),
        "norm_ffn_w": gain(ks[12], (DEPTH, D_MODEL)),
        "w_gate_up": nrm(ks[13], (DEPTH, D_MODEL, 2 * D_FF), D_MODEL),
        "w_down": nrm(ks[14], (DEPTH, D_FF, D_MODEL), D_FF),
        "final_norm_w": gain(ks[15], (D_MODEL,)),
    }


def reference(x, norm_mix_w, w_in, ret_out, conv_w, conv_b, dt_bias, a_log, d_skip, ssd_norm_w,
              ssd_out, w_o, norm_ffn_w, w_gate_up, w_down, final_norm_w):
    h = x
    for l in range(DEPTH):
        h = h + hybrid_mixer(rmsnorm(h, norm_mix_w[l]), w_in[l], ret_out[l], conv_w[l], conv_b[l],
                             dt_bias[l], a_log[l], d_skip[l], ssd_norm_w[l], ssd_out[l], w_o[l])
        h = h + swiglu(rmsnorm(h, norm_ffn_w[l]), w_gate_up[l], w_down[l])
    return rmsnorm(h, final_norm_w)
```

```python
import functools

import jax
import jax.numpy as jnp
from jax import lax
from jax.experimental import pallas as pl
from jax.experimental.pallas import tpu as pltpu

EPS = 1e-6
D_MODEL = 1024

RET_HEADS = 4
RET_DK = 256
RET_DV = 512
RET_QK = RET_HEADS * RET_DK
RET_V = RET_HEADS * RET_DV
ROPE_BASE = 10000.0
ROPE_HALF = RET_DK // 2

SSD_INNER = 2 * D_MODEL
SSD_HEADDIM = 64
SSD_HEADS = SSD_INNER // SSD_HEADDIM
SSD_GROUPS = 4
SSD_HPG = SSD_HEADS // SSD_GROUPS
SSD_STATE = 128
SSD_CONV = 4
SSD_BC = SSD_GROUPS * SSD_STATE
SSD_CONV_CH = SSD_INNER + 2 * SSD_BC

D_FF = 2816

OFF_V = 2 * RET_QK
OFF_DT = OFF_V + 2 * RET_V + SSD_INNER + SSD_CONV_CH
OFF_GATES = OFF_DT + SSD_HEADS
D_IN = OFF_GATES + 2 * D_MODEL

U_COLS = OFF_DT - OFF_V + 2 * D_MODEL

LANES = 128
SSD_CHUNK = 128
RET_CHUNK = 256
RET_BLOCK = 512
HEADS_PER_VREG = LANES // SSD_HEADDIM
SSD_PAIRS = SSD_HEADS // HEADS_PER_VREG
PAIRS_PER_GROUP = SSD_HPG // HEADS_PER_VREG
CONV_PAD = 8

VMEM_LIMIT_BYTES = 56 * 1024 * 1024

BF16 = jnp.bfloat16
F32 = jnp.float32


def _params(*semantics):
    return pltpu.CompilerParams(dimension_semantics=semantics, vmem_limit_bytes=VMEM_LIMIT_BYTES)


def _sigmoid(x):
    return 1.0 / (1.0 + jnp.exp(-x))


def _silu(x):
    return x * _sigmoid(x)


def _softplus(x):
    return jnp.maximum(x, 0.0) + jnp.log(1.0 + jnp.exp(-jnp.abs(x)))


def _rmsnorm_kernel(x_ref, w_ref, o_ref):
    x = x_ref[...]
    y = x * lax.rsqrt(jnp.mean(x * x, axis=-1, keepdims=True) + EPS)
    o_ref[...] = (y * w_ref[...]).astype(o_ref.dtype)


def _rmsnorm(x, w, out_dtype, tm):
    t, d = x.shape
    return pl.pallas_call(
        _rmsnorm_kernel,
        grid=(t // tm,),
        in_specs=[pl.BlockSpec((tm, d), lambda i: (i, 0)), pl.BlockSpec((1, d), lambda i: (0, 0))],
        out_specs=pl.BlockSpec((tm, d), lambda i: (i, 0)),
        out_shape=jax.ShapeDtypeStruct((t, d), out_dtype),
        compiler_params=_params("parallel"),
        name="rmsnorm",
    )(x, w.reshape(1, d))


def _proj_kernel(x_ref, w_ref, o_ref):
    o_ref[...] = jnp.dot(x_ref[...], w_ref[...], preferred_element_type=F32).astype(o_ref.dtype)


def _proj(xn, w, out_dtype, tm, tn, name):
    t, d = xn.shape
    n = w.shape[1]
    return pl.pallas_call(
        _proj_kernel,
        grid=(t // tm, n // tn),
        in_specs=[pl.BlockSpec((tm, d), lambda i, j: (i, 0)), pl.BlockSpec((d, tn), lambda i, j: (0, j))],
        out_specs=pl.BlockSpec((tm, tn), lambda i, j: (i, j)),
        out_shape=jax.ShapeDtypeStruct((t, n), out_dtype),
        compiler_params=_params("parallel", "arbitrary"),
        name=name,
    )(xn, w)


def _qk_kernel(x_ref, w_ref, inv_ref, o_ref, cos_ref, sin_ref, *, tm, tn, seq):
    i = pl.program_id(0)
    j = pl.program_id(1)

    @pl.when(j == 0)
    def _():
        pos0 = lax.rem(i, seq // tm) * tm
        pos = (pos0 + lax.broadcasted_iota(jnp.int32, (tm, ROPE_HALF), 0)).astype(F32)
        ang = pos * inv_ref[...]
        cos_ref[...] = jnp.cos(ang)
        sin_ref[...] = jnp.sin(ang)

    acc = jnp.dot(x_ref[...], w_ref[...], preferred_element_type=F32)
    scale = jnp.where(j >= RET_QK // tn, RET_DK ** -0.5, 1.0).astype(F32)
    cos = cos_ref[...]
    sin = sin_ref[...]
    for h in range(tn // RET_DK):
        lo = h * RET_DK
        t1 = acc[:, lo:lo + ROPE_HALF]
        t2 = acc[:, lo + ROPE_HALF:lo + RET_DK]
        o_ref[:, lo:lo + ROPE_HALF] = ((t1 * cos - t2 * sin) * scale).astype(o_ref.dtype)
        o_ref[:, lo + ROPE_HALF:lo + RET_DK] = ((t1 * sin + t2 * cos) * scale).astype(o_ref.dtype)


def _qk_proj(xn, w_qk, inv, seq, tm, tn):
    t, d = xn.shape
    n = w_qk.shape[1]
    return pl.pallas_call(
        functools.partial(_qk_kernel, tm=tm, tn=tn, seq=seq),
        grid=(t // tm, n // tn),
        in_specs=[pl.BlockSpec((tm, d), lambda i, j: (i, 0)),
                  pl.BlockSpec((d, tn), lambda i, j: (0, j)),
                  pl.BlockSpec((1, ROPE_HALF), lambda i, j: (0, 0))],
        out_specs=pl.BlockSpec((tm, tn), lambda i, j: (i, j)),
        out_shape=jax.ShapeDtypeStruct((t, n), BF16),
        scratch_shapes=[pltpu.VMEM((tm, ROPE_HALF), F32), pltpu.VMEM((tm, ROPE_HALF), F32)],
        compiler_params=_params("parallel", "arbitrary"),
        name="qk_proj_rotary",
    )(xn, w_qk, inv)


def _retention_kernel(q_ref, k_ref, v_ref, g_ref, dmask_ref, qdec_ref, kdec_ref, cdec_ref, o_ref, state_ref):
    @pl.when(pl.program_id(1) == 0)
    def _():
        state_ref[...] = jnp.zeros_like(state_ref)

    contract_last = (((1,), (1,)), ((), ()))
    contract_first = (((0,), (0,)), ((), ()))
    for c in range(RET_BLOCK // RET_CHUNK):
        rows = pl.ds(c * RET_CHUNK, RET_CHUNK)
        for h in range(RET_HEADS):
            q = q_ref[rows, h * RET_DK:(h + 1) * RET_DK]
            k = k_ref[rows, h * RET_DK:(h + 1) * RET_DK]
            v = v_ref[rows, h * RET_DV:(h + 1) * RET_DV]
            s = lax.dot_general(q, k, contract_last, preferred_element_type=F32) * dmask_ref[h]
            st = state_ref[h]
            y = jnp.dot(s.astype(BF16), v, preferred_element_type=F32)
            y = y + jnp.dot(q, st.astype(BF16), preferred_element_type=F32) * qdec_ref[h]
            kd = (k.astype(F32) * kdec_ref[h]).astype(BF16)
            state_ref[h] = st * cdec_ref[h] + lax.dot_general(kd, v, contract_first, preferred_element_type=F32)
            yn = y * lax.rsqrt(jnp.mean(y * y, axis=-1, keepdims=True) + EPS)
            g = g_ref[rows, h * RET_DV:(h + 1) * RET_DV].astype(F32)
            o_ref[rows, h * RET_DV:(h + 1) * RET_DV] = (_silu(g) * yn).astype(o_ref.dtype)


def _retention(qk, u, dmask, qdec, kdec, cdec, batch, seq):
    t = batch * seq
    nblk = seq // RET_BLOCK
    row = lambda b, i: b * nblk + i
    const3 = lambda b, i: (0, 0, 0)
    return pl.pallas_call(
        _retention_kernel,
        grid=(batch, nblk),
        in_specs=[pl.BlockSpec((RET_BLOCK, RET_QK), lambda b, i: (row(b, i), 0)),
                  pl.BlockSpec((RET_BLOCK, RET_QK), lambda b, i: (row(b, i), 1)),
                  pl.BlockSpec((RET_BLOCK, RET_V), lambda b, i: (row(b, i), 0)),
                  pl.BlockSpec((RET_BLOCK, RET_V), lambda b, i: (row(b, i), 1)),
                  pl.BlockSpec((RET_HEADS, RET_CHUNK, RET_CHUNK), const3),
                  pl.BlockSpec((RET_HEADS, RET_CHUNK, 1), const3),
                  pl.BlockSpec((RET_HEADS, RET_CHUNK, 1), const3),
                  pl.BlockSpec((RET_HEADS, 1, 1), const3)],
        out_specs=pl.BlockSpec((RET_BLOCK, RET_V), lambda b, i: (row(b, i), 0)),
        out_shape=jax.ShapeDtypeStruct((t, RET_V), BF16),
        scratch_shapes=[pltpu.VMEM((RET_HEADS, RET_DK, RET_DV), F32)],
        compiler_params=_params("parallel", "arbitrary"),
        name="retention_scan",
    )(qk, qk, u, u, dmask, qdec, kdec, cdec)


def _retention_tables():
    log_g = jnp.log(1.0 - 2.0 ** (-5.0 - jnp.arange(RET_HEADS, dtype=F32)))
    idx = jnp.arange(RET_CHUNK, dtype=F32)
    diff = idx[:, None] - idx[None, :]
    dmask = jnp.where((diff >= 0)[None], jnp.exp(log_g[:, None, None] * jnp.maximum(diff, 0.0)[None]), 0.0)
    qdec = jnp.exp(log_g[:, None] * (idx + 1.0))[:, :, None]
    kdec = jnp.exp(log_g[:, None] * (RET_CHUNK - 1.0 - idx))[:, :, None]
    cdec = jnp.exp(log_g * RET_CHUNK)[:, None, None]
    return dmask, qdec, kdec, cdec


def _cumsum_rows(x):
    n = x.shape[0]
    row = lax.broadcasted_iota(jnp.int32, x.shape, 0)
    shift = 1
    while shift < n:
        x = x + jnp.where(row >= shift, pltpu.roll(x, shift, 0), 0.0)
        shift *= 2
    return x


def _ssd_kernel(xbc_ref, z_ref, dt_ref, cw_ref, cb_ref, dtb_ref, alog_ref, dskip_ref, nw_ref,
                o_ref, win_ref, state_ref, y_ref):
    c = SSD_CHUNK

    @pl.when(pl.program_id(1) == 0)
    def _():
        state_ref[...] = jnp.zeros_like(state_ref)
        win_ref[0:CONV_PAD, :] = jnp.zeros((CONV_PAD, SSD_CONV_CH), F32)

    win_ref[CONV_PAD:CONV_PAD + c, :] = xbc_ref[...].astype(F32)
    conv = cb_ref[...]
    for tap in range(SSD_CONV):
        lo = CONV_PAD - (SSD_CONV - 1) + tap
        conv = conv + win_ref[lo:lo + c, :] * cw_ref[tap:tap + 1, :]
    win_ref[0:CONV_PAD, :] = win_ref[c:c + CONV_PAD, :]
    xc = _silu(conv)

    xs_bf = xc[:, :SSD_INNER].astype(BF16)
    dtp = _softplus(dt_ref[...] + dtb_ref[...])
    la = dtp * (-jnp.exp(alog_ref[...]))
    acs = _cumsum_rows(la)
    acs_t = acs.T
    dtp_t = dtp.T

    row = lax.broadcasted_iota(jnp.int32, (c, c), 0)
    col = lax.broadcasted_iota(jnp.int32, (c, c), 1)
    causal = row >= col
    left = lax.broadcasted_iota(jnp.int32, (1, LANES), 1) < SSD_HEADDIM
    contract_last = (((1,), (1,)), ((), ()))

    for g in range(SSD_GROUPS):
        b_g = xc[:, SSD_INNER + g * SSD_STATE:SSD_INNER + (g + 1) * SSD_STATE]
        c_g = xc[:, SSD_INNER + SSD_BC + g * SSD_STATE:SSD_INNER + SSD_BC + (g + 1) * SSD_STATE]
        b_bf = b_g.astype(BF16)
        c_bf = c_g.astype(BF16)
        cb_g = lax.dot_general(c_bf, b_bf, contract_last, preferred_element_type=F32)
        b_t = b_bf.astype(F32).T
        c_f = c_bf.astype(F32)
        for pp in range(PAIRS_PER_GROUP):
            p = g * PAIRS_PER_GROUP + pp
            lhs_y, lhs_s, decs = [], [], []
            for r in (HEADS_PER_VREG * p, HEADS_PER_VREG * p + 1):
                a_i = jnp.broadcast_to(acs[:, r:r + 1], (c, c))
                a_j = jnp.broadcast_to(acs_t[r:r + 1, :], (c, c))
                dt_j = dtp_t[r:r + 1, :]
                lmat = jnp.exp(jnp.where(causal, a_i - a_j, -1e30))
                lhs_y.append((cb_g * lmat * dt_j).astype(BF16))
                lhs_y.append((c_f * jnp.exp(a_i)).astype(BF16))
                a_last = acs_t[r:r + 1, c - 1:c]
                w_j = dt_j * jnp.exp(a_last - acs_t[r:r + 1, :])
                lhs_s.append((b_t * w_j).astype(BF16))
                decs.append(jnp.exp(a_last))
            xs_p = xs_bf[:, p * LANES:(p + 1) * LANES]
            zero_x = jnp.zeros_like(xs_p)
            xs_l = jnp.where(left, xs_p, zero_x)
            xs_r = jnp.where(left, zero_x, xs_p)
            st = state_ref[p]
            st_bf = st.astype(BF16)
            zero_s = jnp.zeros_like(st_bf)
            st_l = jnp.where(left, st_bf, zero_s)
            st_r = jnp.where(left, zero_s, st_bf)
            y = jnp.dot(jnp.concatenate(lhs_y, axis=1), jnp.concatenate([xs_l, st_l, xs_r, st_r], axis=0),
                        preferred_element_type=F32)
            upd = jnp.dot(jnp.concatenate(lhs_s, axis=1), jnp.concatenate([xs_l, xs_r], axis=0),
                          preferred_element_type=F32)
            dec = jnp.where(left, decs[0], decs[1])
            state_ref[p] = st * dec + upd
            y_ref[:, p * LANES:(p + 1) * LANES] = y

    gw = SSD_INNER // SSD_GROUPS
    for g in range(SSD_GROUPS):
        cols = slice(g * gw, (g + 1) * gw)
        y = y_ref[:, cols]
        y = y + dskip_ref[:, cols] * xc[:, cols]
        y = y * _silu(z_ref[:, cols].astype(F32))
        y = y * lax.rsqrt(jnp.mean(y * y, axis=-1, keepdims=True) + EPS)
        o_ref[:, cols] = (y * nw_ref[:, cols]).astype(o_ref.dtype)


def _ssd(u, dt, conv_w, conv_b, dt_bias, a_log, d_skip, norm_w, batch, seq):
    t = batch * seq
    c = SSD_CHUNK
    nblk = seq // c
    row = lambda b, i: b * nblk + i
    const = lambda b, i: (0, 0)
    xbc_blk = (2 * RET_V + SSD_INNER) // SSD_CONV_CH
    z_blk = 2 * RET_V // SSD_INNER
    pad = LANES - SSD_HEADS
    return pl.pallas_call(
        _ssd_kernel,
        grid=(batch, nblk),
        in_specs=[pl.BlockSpec((c, SSD_CONV_CH), lambda b, i: (row(b, i), xbc_blk)),
                  pl.BlockSpec((c, SSD_INNER), lambda b, i: (row(b, i), z_blk)),
                  pl.BlockSpec((c, LANES), lambda b, i: (row(b, i), 0)),
                  pl.BlockSpec((SSD_CONV, SSD_CONV_CH), const),
                  pl.BlockSpec((1, SSD_CONV_CH), const),
                  pl.BlockSpec((1, LANES), const),
                  pl.BlockSpec((1, LANES), const),
                  pl.BlockSpec((1, SSD_INNER), const),
                  pl.BlockSpec((1, SSD_INNER), const)],
        out_specs=pl.BlockSpec((c, SSD_INNER), lambda b, i: (row(b, i), 0)),
        out_shape=jax.ShapeDtypeStruct((t, SSD_INNER), BF16),
        scratch_shapes=[pltpu.VMEM((c + CONV_PAD, SSD_CONV_CH), F32),
                        pltpu.VMEM((SSD_PAIRS, SSD_STATE, LANES), F32),
                        pltpu.VMEM((c, SSD_INNER), F32)],
        compiler_params=_params("parallel", "arbitrary"),
        name="ssd_scan",
    )(u, u, dt, conv_w, conv_b.reshape(1, -1),
      jnp.pad(dt_bias, (0, pad)).reshape(1, LANES), jnp.pad(a_log, (0, pad)).reshape(1, LANES),
      jnp.repeat(d_skip, SSD_HEADDIM).reshape(1, SSD_INNER), norm_w.reshape(1, SSD_INNER))


def _merge_kernel(yr_ref, ys_ref, gr_ref, gs_ref, h_ref, wr_ref, ws_ref, wo_ref, o_ref):
    o_ret = jnp.dot(yr_ref[...], wr_ref[...], preferred_element_type=F32)
    o_ssd = jnp.dot(ys_ref[...], ws_ref[...], preferred_element_type=F32)
    merged = _sigmoid(gr_ref[...].astype(F32)) * o_ret + _sigmoid(gs_ref[...].astype(F32)) * o_ssd
    o_ref[...] = h_ref[...] + jnp.dot(merged.astype(BF16), wo_ref[...], preferred_element_type=F32)


def _merge(yr, ys, u, h, w_ret, w_ssd, w_o, tm):
    t, d = h.shape
    gate_blk = (U_COLS - 2 * D_MODEL) // D_MODEL
    rows = lambda i: (i, 0)
    const = lambda i: (0, 0)
    return pl.pallas_call(
        _merge_kernel,
        grid=(t // tm,),
        in_specs=[pl.BlockSpec((tm, RET_V), rows),
                  pl.BlockSpec((tm, SSD_INNER), rows),
                  pl.BlockSpec((tm, d), lambda i: (i, gate_blk)),
                  pl.BlockSpec((tm, d), lambda i: (i, gate_blk + 1)),
                  pl.BlockSpec((tm, d), rows),
                  pl.BlockSpec((RET_V, d), const),
                  pl.BlockSpec((SSD_INNER, d), const),
                  pl.BlockSpec((d, d), const)],
        out_specs=pl.BlockSpec((tm, d), rows),
        out_shape=jax.ShapeDtypeStruct((t, d), F32),
        compiler_params=_params("parallel"),
        name="out_proj_merge",
    )(yr, ys, u, u, h, w_ret, w_ssd, w_o)


def _ffn_kernel(h_ref, nw_ref, wg_ref, wu_ref, wd_ref, nw2_ref, o_ref, on_ref, xn_ref, acc_ref):
    j = pl.program_id(1)

    @pl.when(j == 0)
    def _():
        x = h_ref[...]
        y = x * lax.rsqrt(jnp.mean(x * x, axis=-1, keepdims=True) + EPS)
        xn_ref[...] = (y * nw_ref[...]).astype(BF16)
        acc_ref[...] = jnp.zeros_like(acc_ref)

    xn = xn_ref[...]
    gate = jnp.dot(xn, wg_ref[...], preferred_element_type=F32)
    up = jnp.dot(xn, wu_ref[...], preferred_element_type=F32)
    acc_ref[...] += jnp.dot((_silu(gate) * up).astype(BF16), wd_ref[...], preferred_element_type=F32)

    @pl.when(j == pl.num_programs(1) - 1)
    def _():
        out = h_ref[...] + acc_ref[...]
        o_ref[...] = out
        y = out * lax.rsqrt(jnp.mean(out * out, axis=-1, keepdims=True) + EPS)
        on_ref[...] = (y * nw2_ref[...]).astype(on_ref.dtype)


def _ffn(h, norm_w, w_gate_up, w_down, next_norm_w, next_dtype, tm, tf):
    t, d = h.shape
    nff = D_FF // tf
    rows = lambda i, j: (i, 0)
    const = lambda i, j: (0, 0)
    return pl.pallas_call(
        _ffn_kernel,
        grid=(t // tm, nff),
        in_specs=[pl.BlockSpec((tm, d), rows),
                  pl.BlockSpec((1, d), const),
                  pl.BlockSpec((d, tf), lambda i, j: (0, j)),
                  pl.BlockSpec((d, tf), lambda i, j: (0, nff + j)),
                  pl.BlockSpec((tf, d), lambda i, j: (j, 0)),
                  pl.BlockSpec((1, d), const)],
        out_specs=[pl.BlockSpec((tm, d), rows), pl.BlockSpec((tm, d), rows)],
        out_shape=[jax.ShapeDtypeStruct((t, d), F32), jax.ShapeDtypeStruct((t, d), next_dtype)],
        scratch_shapes=[pltpu.VMEM((tm, d), BF16), pltpu.VMEM((tm, d), F32)],
        compiler_params=_params("parallel", "arbitrary"),
        name="swiglu_ffn",
    )(h, norm_w.reshape(1, d), w_gate_up, w_gate_up, w_down, next_norm_w.reshape(1, d))


def kernel(x, norm_mix_w, w_in, ret_out, conv_w, conv_b, dt_bias, a_log, d_skip, ssd_norm_w, ssd_out, w_o,
           norm_ffn_w, w_gate_up, w_down, final_norm_w):
    batch, seq, d = x.shape
    depth = w_in.shape[0]
    assert d == D_MODEL and w_in.shape[2] == D_IN
    assert seq % RET_BLOCK == 0 and seq % SSD_CHUNK == 0
    t = batch * seq
    tm = min(1024, seq)

    inv = (ROPE_BASE ** (-jnp.arange(ROPE_HALF, dtype=F32) / ROPE_HALF)).reshape(1, ROPE_HALF)
    dmask, qdec, kdec, cdec = _retention_tables()

    h = x.reshape(t, d)
    xn = _rmsnorm(h, norm_mix_w[0], BF16, tm)
    out = None
    for l in range(depth):
        w = w_in[l]
        w_qk = w[:, :OFF_V].astype(BF16)
        w_u = jnp.concatenate([w[:, OFF_V:OFF_DT], w[:, OFF_GATES:]], axis=1).astype(BF16)
        w_dt = jnp.pad(w[:, OFF_DT:OFF_GATES], ((0, 0), (0, LANES - SSD_HEADS))).astype(BF16)

        qk = _qk_proj(xn, w_qk, inv, seq, tm, 512)
        u = _proj(xn, w_u, BF16, tm, 512, "in_proj")
        dt = _proj(xn, w_dt, F32, tm, LANES, "dt_proj")

        yr = _retention(qk, u, dmask, qdec, kdec, cdec, batch, seq)
        ys = _ssd(u, dt, conv_w[l], conv_b[l], dt_bias[l], a_log[l], d_skip[l], ssd_norm_w[l], batch, seq)
        h = _merge(yr, ys, u, h, ret_out[l].astype(BF16), ssd_out[l].astype(BF16), w_o[l].astype(BF16), 512)

        last = l == depth - 1
        next_w = final_norm_w if last else norm_mix_w[l + 1]
        h, nxt = _ffn(h, norm_ffn_w[l], w_gate_up[l].astype(BF16), w_down[l].astype(BF16), next_w,
                      F32 if last else BF16, tm, 256)
        if last:
            out = nxt
        else:
            xn = nxt
    return out.reshape(batch, seq, d)
```

```python
import functools

import jax
import jax.numpy as jnp
from jax import lax
from jax.experimental import pallas as pl
from jax.experimental.pallas import tpu as pltpu

EPS = 1e-6
D_MODEL = 1024

RET_HEADS = 4
RET_DK = 256
RET_DV = 512
RET_QK = RET_HEADS * RET_DK
RET_V = RET_HEADS * RET_DV
ROPE_BASE = 10000.0
ROPE_HALF = RET_DK // 2

SSD_INNER = 2 * D_MODEL
SSD_HEADDIM = 64
SSD_HEADS = SSD_INNER // SSD_HEADDIM
SSD_GROUPS = 4
SSD_HPG = SSD_HEADS // SSD_GROUPS
SSD_STATE = 128
SSD_CONV = 4
SSD_BC = SSD_GROUPS * SSD_STATE
SSD_CONV_CH = SSD_INNER + 2 * SSD_BC

D_FF = 2816

OFF_V = 2 * RET_QK
OFF_G = OFF_V + RET_V
OFF_XBC = OFF_G + RET_V + SSD_INNER
OFF_DT = OFF_XBC + SSD_CONV_CH
OFF_GATES = OFF_DT + SSD_HEADS
D_IN = OFF_GATES + 2 * D_MODEL

LANES = 128
SUBLANES = 8
MXU_COLS = 256
SSD_CHUNK = 128
RET_CHUNK = 256
RET_BLOCK = 512
HEADS_PER_VREG = LANES // SSD_HEADDIM
SSD_PAIRS = SSD_HEADS // HEADS_PER_VREG
PAIRS_PER_GROUP = SSD_HPG // HEADS_PER_VREG

PROJ_TM = 1024
PROJ_TN = 1024
MERGE_TM = 512
FFN_TM = 512

VMEM_LIMIT_BYTES = 56 * 1024 * 1024

BF16 = jnp.bfloat16
F32 = jnp.float32


def _params(*semantics):
    return pltpu.CompilerParams(dimension_semantics=semantics, vmem_limit_bytes=VMEM_LIMIT_BYTES)


def _resident(shape):
    return pl.BlockSpec(shape, lambda *_: (0,) * len(shape), pipeline_mode=pl.Buffered(1))


def _sigmoid(x):
    return 1.0 / (1.0 + jnp.exp(-x))


def _silu(x):
    return x * _sigmoid(x)


def _softplus(x):
    return jnp.maximum(x, 0.0) + jnp.log(1.0 + jnp.exp(-jnp.abs(x)))


def _rms(x, w):
    return x * lax.rsqrt(jnp.mean(x * x, axis=-1, keepdims=True) + EPS) * w


_ACTIVATIONS = {"none": lambda a: a, "silu": _silu, "sigmoid": _sigmoid}


def _rmsnorm_kernel(x_ref, w_ref, o_ref):
    o_ref[...] = _rms(x_ref[...], w_ref[...]).astype(o_ref.dtype)


def _rmsnorm(x, w, out_dtype, tm):
    t, d = x.shape
    return pl.pallas_call(
        _rmsnorm_kernel,
        grid=(t // tm,),
        in_specs=[pl.BlockSpec((tm, d), lambda i: (i, 0)), pl.BlockSpec((1, d), lambda i: (0, 0))],
        out_specs=pl.BlockSpec((tm, d), lambda i: (i, 0)),
        out_shape=jax.ShapeDtypeStruct((t, d), out_dtype),
        compiler_params=_params("parallel"),
        name="rmsnorm",
    )(x, w.reshape(1, d))


def _rope_kernel(inv_ref, cos_ref, sin_ref, *, tm):
    pos = (pl.program_id(0) * tm + lax.broadcasted_iota(jnp.int32, (tm, ROPE_HALF), 0)).astype(F32)
    ang = pos * inv_ref[...]
    cos_ref[...] = jnp.cos(ang)
    sin_ref[...] = jnp.sin(ang)


def _rope_table(inv, seq, tm):
    spec = pl.BlockSpec((tm, ROPE_HALF), lambda i: (i, 0))
    shape = jax.ShapeDtypeStruct((seq, ROPE_HALF), F32)
    return pl.pallas_call(
        functools.partial(_rope_kernel, tm=tm),
        grid=(seq // tm,),
        in_specs=[pl.BlockSpec((1, ROPE_HALF), lambda i: (0, 0))],
        out_specs=[spec, spec],
        out_shape=[shape, shape],
        compiler_params=_params("parallel"),
        name="rope_table",
    )(inv)


def _proj_kernel(x_ref, w_ref, o_ref, *, act):
    acc = jnp.dot(x_ref[...], w_ref[...], preferred_element_type=F32)
    o_ref[...] = _ACTIVATIONS[act](acc).astype(o_ref.dtype)


def _proj(xn, w, col0, ncols, act, out_dtype, tm, tn, name):
    t, d = xn.shape
    blk0 = col0 // tn
    return pl.pallas_call(
        functools.partial(_proj_kernel, act=act),
        grid=(t // tm, ncols // tn),
        in_specs=[pl.BlockSpec((tm, d), lambda i, j: (i, 0)), pl.BlockSpec((d, tn), lambda i, j: (0, blk0 + j))],
        out_specs=pl.BlockSpec((tm, tn), lambda i, j: (i, j)),
        out_shape=jax.ShapeDtypeStruct((t, ncols), out_dtype),
        compiler_params=_params("parallel", "arbitrary"),
        name=name,
    )(xn, w)


def _qk_kernel(x_ref, w_ref, cos_ref, sin_ref, o_ref, *, tn):
    acc = jnp.dot(x_ref[...], w_ref[...], preferred_element_type=F32)
    scale = jnp.where(pl.program_id(1) >= RET_QK // tn, RET_DK ** -0.5, 1.0).astype(F32)
    cos = cos_ref[...] * scale
    sin = sin_ref[...] * scale
    for h in range(tn // RET_DK):
        lo = h * RET_DK
        t1 = acc[:, lo:lo + ROPE_HALF]
        t2 = acc[:, lo + ROPE_HALF:lo + RET_DK]
        o_ref[:, lo:lo + ROPE_HALF] = (t1 * cos - t2 * sin).astype(o_ref.dtype)
        o_ref[:, lo + ROPE_HALF:lo + RET_DK] = (t1 * sin + t2 * cos).astype(o_ref.dtype)


def _qk_proj(xn, w, cos, sin, seq, tm, tn):
    t, d = xn.shape
    npos = seq // tm
    return pl.pallas_call(
        functools.partial(_qk_kernel, tn=tn),
        grid=(t // tm, 2 * RET_QK // tn),
        in_specs=[pl.BlockSpec((tm, d), lambda i, j: (i, 0)),
                  pl.BlockSpec((d, tn), lambda i, j: (0, j)),
                  pl.BlockSpec((tm, ROPE_HALF), lambda i, j: (i % npos, 0)),
                  pl.BlockSpec((tm, ROPE_HALF), lambda i, j: (i % npos, 0))],
        out_specs=pl.BlockSpec((tm, tn), lambda i, j: (i, j)),
        out_shape=jax.ShapeDtypeStruct((t, 2 * RET_QK), BF16),
        compiler_params=_params("parallel", "arbitrary"),
        name="qk_proj_rotary",
    )(xn, w, cos, sin)


def _shift_rows(a, prev, back):
    n, width = a.shape
    rot = pltpu.roll(a.reshape(n // SUBLANES, SUBLANES, width), back, 1)
    below = jnp.concatenate([pltpu.roll(prev, back, 0)[None], rot[:-1]], axis=0)
    sub = lax.broadcasted_iota(jnp.int32, rot.shape, 1)
    return jnp.where(sub < back, below, rot).reshape(n, width)


def _vxbc_kernel(x_ref, wv_ref, wx_ref, cw_ref, cb_ref, ov_ref, ox_ref, carry_ref, *, tm, seq):
    i = pl.program_id(0)
    j = pl.program_id(1)

    @pl.when(i == 0)
    def _():
        carry_ref[j] = jnp.zeros(carry_ref.shape[1:], F32)

    x = x_ref[...]
    batch_start = lax.rem(i, seq // tm) == 0
    nv = ov_ref.shape[1] // MXU_COLS
    for s in range(ox_ref.shape[1] // MXU_COLS):
        cols = slice(s * MXU_COLS, (s + 1) * MXU_COLS)
        acc = jnp.dot(x, wx_ref[:, cols], preferred_element_type=F32)
        if s < nv:
            ov_ref[:, cols] = jnp.dot(x, wv_ref[:, cols], preferred_element_type=F32).astype(ov_ref.dtype)
        prev = jnp.where(batch_start, 0.0, carry_ref[j, :, cols])
        carry_ref[j, :, cols] = acc[tm - SUBLANES:, :]
        conv = cb_ref[:, cols] + acc * cw_ref[SSD_CONV - 1:SSD_CONV, cols]
        for back in range(1, SSD_CONV):
            k = SSD_CONV - 1 - back
            conv = conv + _shift_rows(acc, prev, back) * cw_ref[k:k + 1, cols]
        ox_ref[:, cols] = _silu(conv).astype(ox_ref.dtype)


def _vxbc_proj(xn, w, w_xbc, conv_w, conv_b, seq, tm, nsplit):
    t, d = xn.shape
    tnv = RET_V // nsplit
    tnx = SSD_CONV_CH // nsplit
    vblk0 = OFF_V // tnv
    return pl.pallas_call(
        functools.partial(_vxbc_kernel, tm=tm, seq=seq),
        grid=(t // tm, nsplit),
        in_specs=[pl.BlockSpec((tm, d), lambda i, j: (i, 0)),
                  pl.BlockSpec((d, tnv), lambda i, j: (0, vblk0 + j)),
                  pl.BlockSpec((d, tnx), lambda i, j: (0, j)),
                  pl.BlockSpec((SSD_CONV, tnx), lambda i, j: (0, j)),
                  pl.BlockSpec((1, tnx), lambda i, j: (0, j))],
        out_specs=[pl.BlockSpec((tm, tnv), lambda i, j: (i, j)), pl.BlockSpec((tm, tnx), lambda i, j: (i, j))],
        out_shape=[jax.ShapeDtypeStruct((t, RET_V), BF16), jax.ShapeDtypeStruct((t, SSD_CONV_CH), BF16)],
        scratch_shapes=[pltpu.VMEM((nsplit, SUBLANES, tnx), F32)],
        compiler_params=_params("arbitrary", "arbitrary"),
        name="v_xbc_proj_conv",
    )(xn, w, w_xbc, conv_w, conv_b.reshape(1, -1))


def _retention_kernel(q_ref, k_ref, v_ref, g_ref, dmask_ref, qdec_ref, kdec_ref, cdec_ref, o_ref, state_ref):
    @pl.when(pl.program_id(1) == 0)
    def _():
        state_ref[...] = jnp.zeros_like(state_ref)

    contract_last = (((1,), (1,)), ((), ()))
    contract_first = (((0,), (0,)), ((), ()))
    for c in range(RET_BLOCK // RET_CHUNK):
        rows = pl.ds(c * RET_CHUNK, RET_CHUNK)
        for h in range(RET_HEADS):
            q = q_ref[rows, h * RET_DK:(h + 1) * RET_DK]
            k = k_ref[rows, h * RET_DK:(h + 1) * RET_DK]
            v = v_ref[rows, h * RET_DV:(h + 1) * RET_DV]
            s = lax.dot_general(q, k, contract_last, preferred_element_type=F32) * dmask_ref[h]
            st = state_ref[h]
            y = jnp.dot(s.astype(BF16), v, preferred_element_type=F32)
            y = y + jnp.dot(q, st.astype(BF16), preferred_element_type=F32) * qdec_ref[h]
            kd = (k.astype(F32) * kdec_ref[h]).astype(BF16)
            state_ref[h] = st * cdec_ref[h] + lax.dot_general(kd, v, contract_first, preferred_element_type=F32)
            yn = y * lax.rsqrt(jnp.mean(y * y, axis=-1, keepdims=True) + EPS)
            gate = g_ref[rows, h * RET_DV:(h + 1) * RET_DV].astype(F32)
            o_ref[rows, h * RET_DV:(h + 1) * RET_DV] = (gate * yn).astype(o_ref.dtype)


def _retention(qk, v, gz, dmask, qdec, kdec, cdec, batch, seq):
    t = batch * seq
    nblk = seq // RET_BLOCK
    row = lambda b, i: b * nblk + i
    return pl.pallas_call(
        _retention_kernel,
        grid=(batch, nblk),
        in_specs=[pl.BlockSpec((RET_BLOCK, RET_QK), lambda b, i: (row(b, i), 0)),
                  pl.BlockSpec((RET_BLOCK, RET_QK), lambda b, i: (row(b, i), 1)),
                  pl.BlockSpec((RET_BLOCK, RET_V), lambda b, i: (row(b, i), 0)),
                  pl.BlockSpec((RET_BLOCK, RET_V), lambda b, i: (row(b, i), 0)),
                  _resident((RET_HEADS, RET_CHUNK, RET_CHUNK)),
                  _resident((RET_HEADS, RET_CHUNK, 1)),
                  _resident((RET_HEADS, RET_CHUNK, 1)),
                  _resident((RET_HEADS, 1, 1))],
        out_specs=pl.BlockSpec((RET_BLOCK, RET_V), lambda b, i: (row(b, i), 0)),
        out_shape=jax.ShapeDtypeStruct((t, RET_V), BF16),
        scratch_shapes=[pltpu.VMEM((RET_HEADS, RET_DK, RET_DV), F32)],
        compiler_params=_params("parallel", "arbitrary"),
        name="retention_scan",
    )(qk, qk, v, gz, dmask, qdec, kdec, cdec)


def _retention_tables():
    log_g = jnp.log(1.0 - 2.0 ** (-5.0 - jnp.arange(RET_HEADS, dtype=F32)))
    idx = jnp.arange(RET_CHUNK, dtype=F32)
    diff = idx[:, None] - idx[None, :]
    dmask = jnp.where((diff >= 0)[None], jnp.exp(log_g[:, None, None] * jnp.maximum(diff, 0.0)[None]), 0.0)
    qdec = jnp.exp(log_g[:, None] * (idx + 1.0))[:, :, None]
    kdec = jnp.exp(log_g[:, None] * (RET_CHUNK - 1.0 - idx))[:, :, None]
    cdec = jnp.exp(log_g * RET_CHUNK)[:, None, None]
    return dmask, qdec, kdec, cdec


def _cumsum_rows(x):
    n = x.shape[0]
    row = lax.broadcasted_iota(jnp.int32, x.shape, 0)
    shift = 1
    while shift < n:
        x = x + jnp.where(row >= shift, pltpu.roll(x, shift, 0), 0.0)
        shift *= 2
    return x


def _ssd_kernel(xc_ref, sz_ref, dt_ref, dtb_ref, alog_ref, dskip_ref, nw_ref, o_ref, state_ref, y_ref):
    c = SSD_CHUNK

    @pl.when(pl.program_id(1) == 0)
    def _():
        state_ref[...] = jnp.zeros_like(state_ref)

    dtp = _softplus(dt_ref[...] + dtb_ref[...])
    la = dtp * (-jnp.exp(alog_ref[...]))
    acs = _cumsum_rows(la)
    acs_t = acs.T
    dtp_t = dtp.T

    row = lax.broadcasted_iota(jnp.int32, (c, c), 0)
    col = lax.broadcasted_iota(jnp.int32, (c, c), 1)
    causal = row >= col
    left = lax.broadcasted_iota(jnp.int32, (1, LANES), 1) < SSD_HEADDIM
    contract_last = (((1,), (1,)), ((), ()))

    for g in range(SSD_GROUPS):
        b_bf = xc_ref[:, SSD_INNER + g * SSD_STATE:SSD_INNER + (g + 1) * SSD_STATE]
        c_bf = xc_ref[:, SSD_INNER + SSD_BC + g * SSD_STATE:SSD_INNER + SSD_BC + (g + 1) * SSD_STATE]
        cb_g = lax.dot_general(c_bf, b_bf, contract_last, preferred_element_type=F32)
        b_t = b_bf.astype(F32).T
        c_f = c_bf.astype(F32)
        for pp in range(PAIRS_PER_GROUP):
            p = g * PAIRS_PER_GROUP + pp
            lhs_y, lhs_s, decs = [], [], []
            for r in (HEADS_PER_VREG * p, HEADS_PER_VREG * p + 1):
                a_i = jnp.broadcast_to(acs[:, r:r + 1], (c, c))
                a_j = jnp.broadcast_to(acs_t[r:r + 1, :], (c, c))
                dt_j = dtp_t[r:r + 1, :]
                lmat = jnp.exp(jnp.where(causal, a_i - a_j, -1e30))
                lhs_y.append((cb_g * lmat * dt_j).astype(BF16))
                lhs_y.append((c_f * jnp.exp(a_i)).astype(BF16))
                a_last = acs_t[r:r + 1, c - 1:c]
                w_j = dt_j * jnp.exp(a_last - acs_t[r:r + 1, :])
                lhs_s.append((b_t * w_j).astype(BF16))
                decs.append(jnp.exp(a_last))
            xs_p = xc_ref[:, p * LANES:(p + 1) * LANES]
            zero_x = jnp.zeros_like(xs_p)
            xs_l = jnp.where(left, xs_p, zero_x)
            xs_r = jnp.where(left, zero_x, xs_p)
            st = state_ref[p]
            st_bf = st.astype(BF16)
            zero_s = jnp.zeros_like(st_bf)
            st_l = jnp.where(left, st_bf, zero_s)
            st_r = jnp.where(left, zero_s, st_bf)
            y = jnp.dot(jnp.concatenate(lhs_y, axis=1), jnp.concatenate([xs_l, st_l, xs_r, st_r], axis=0),
                        preferred_element_type=F32)
            upd = jnp.dot(jnp.concatenate(lhs_s, axis=1), jnp.concatenate([xs_l, xs_r], axis=0),
                          preferred_element_type=F32)
            dec = jnp.where(left, decs[0], decs[1])
            state_ref[p] = st * dec + upd
            y_ref[:, p * LANES:(p + 1) * LANES] = y

    gw = SSD_INNER // SSD_GROUPS
    for g in range(SSD_GROUPS):
        cols = slice(g * gw, (g + 1) * gw)
        y = y_ref[:, cols] + dskip_ref[:, cols] * xc_ref[:, cols].astype(F32)
        y = y * sz_ref[:, cols].astype(F32)
        o_ref[:, cols] = _rms(y, nw_ref[:, cols]).astype(o_ref.dtype)


def _ssd(xc, gz, dt, dt_bias, a_log, d_skip, norm_w, batch, seq):
    t = batch * seq
    c = SSD_CHUNK
    nblk = seq // c
    row = lambda b, i: b * nblk + i
    pad = LANES - SSD_HEADS
    return pl.pallas_call(
        _ssd_kernel,
        grid=(batch, nblk),
        in_specs=[pl.BlockSpec((c, SSD_CONV_CH), lambda b, i: (row(b, i), 0)),
                  pl.BlockSpec((c, SSD_INNER), lambda b, i: (row(b, i), 1)),
                  pl.BlockSpec((c, LANES), lambda b, i: (row(b, i), 0)),
                  _resident((1, LANES)),
                  _resident((1, LANES)),
                  _resident((1, SSD_INNER)),
                  _resident((1, SSD_INNER))],
        out_specs=pl.BlockSpec((c, SSD_INNER), lambda b, i: (row(b, i), 0)),
        out_shape=jax.ShapeDtypeStruct((t, SSD_INNER), BF16),
        scratch_shapes=[pltpu.VMEM((SSD_PAIRS, SSD_STATE, LANES), F32),
                        pltpu.VMEM((c, SSD_INNER), F32)],
        compiler_params=_params("parallel", "arbitrary"),
        name="ssd_scan",
    )(xc, gz, dt,
      jnp.pad(dt_bias, (0, pad)).reshape(1, LANES), jnp.pad(a_log, (0, pad)).reshape(1, LANES),
      jnp.repeat(d_skip, SSD_HEADDIM).reshape(1, SSD_INNER), norm_w.reshape(1, SSD_INNER))


def _merge_kernel(yr_ref, ys_ref, gr_ref, gs_ref, h_ref, wr_ref, ws_ref, wo_ref, o_ref):
    o_ret = jnp.dot(yr_ref[...], wr_ref[...], preferred_element_type=F32)
    o_ssd = jnp.dot(ys_ref[...], ws_ref[...], preferred_element_type=F32)
    merged = gr_ref[...].astype(F32) * o_ret + gs_ref[...].astype(F32) * o_ssd
    o_ref[...] = h_ref[...] + jnp.dot(merged.astype(BF16), wo_ref[...], preferred_element_type=F32)


def _merge(yr, ys, gates, h, w_ret, w_ssd, w_o, tm):
    t, d = h.shape
    rows = lambda i: (i, 0)
    return pl.pallas_call(
        _merge_kernel,
        grid=(t // tm,),
        in_specs=[pl.BlockSpec((tm, RET_V), rows),
                  pl.BlockSpec((tm, SSD_INNER), rows),
                  pl.BlockSpec((tm, d), lambda i: (i, 0)),
                  pl.BlockSpec((tm, d), lambda i: (i, 1)),
                  pl.BlockSpec((tm, d), rows),
                  _resident((RET_V, d)),
                  _resident((SSD_INNER, d)),
                  _resident((d, d))],
        out_specs=pl.BlockSpec((tm, d), rows),
        out_shape=jax.ShapeDtypeStruct((t, d), F32),
        compiler_params=_params("parallel"),
        name="out_proj_merge",
    )(yr, ys, gates, gates, h, w_ret, w_ssd, w_o)


def _ffn_kernel(h_ref, nw_ref, wgu_ref, wd_ref, nw2_ref, o_ref, on_ref):
    x = h_ref[...]
    xn = _rms(x, nw_ref[...]).astype(BF16)
    gate = jnp.dot(xn, wgu_ref[:, :D_FF], preferred_element_type=F32)
    up = jnp.dot(xn, wgu_ref[:, D_FF:], preferred_element_type=F32)
    out = x + jnp.dot((_silu(gate) * up).astype(BF16), wd_ref[...], preferred_element_type=F32)
    o_ref[...] = out
    on_ref[...] = _rms(out, nw2_ref[...]).astype(on_ref.dtype)


def _ffn(h, norm_w, w_gate_up, w_down, next_norm_w, next_dtype, tm):
    t, d = h.shape
    rows = lambda i: (i, 0)
    return pl.pallas_call(
        _ffn_kernel,
        grid=(t // tm,),
        in_specs=[pl.BlockSpec((tm, d), rows),
                  _resident((1, d)),
                  _resident((d, 2 * D_FF)),
                  _resident((D_FF, d)),
                  _resident((1, d))],
        out_specs=[pl.BlockSpec((tm, d), rows), pl.BlockSpec((tm, d), rows)],
        out_shape=[jax.ShapeDtypeStruct((t, d), F32), jax.ShapeDtypeStruct((t, d), next_dtype)],
        compiler_params=_params("parallel"),
        name="swiglu_ffn",
    )(h, norm_w.reshape(1, d), w_gate_up, w_down, next_norm_w.reshape(1, d))


def kernel(x, norm_mix_w, w_in, ret_out, conv_w, conv_b, dt_bias, a_log, d_skip, ssd_norm_w, ssd_out, w_o,
           norm_ffn_w, w_gate_up, w_down, final_norm_w):
    batch, seq, d = x.shape
    depth = w_in.shape[0]
    assert d == D_MODEL and w_in.shape[2] == D_IN
    assert seq % RET_BLOCK == 0 and seq % SSD_CHUNK == 0
    t = batch * seq
    tm = min(PROJ_TM, seq)
    tn = PROJ_TN

    inv = (ROPE_BASE ** (-jnp.arange(ROPE_HALF, dtype=F32) / ROPE_HALF)).reshape(1, ROPE_HALF)
    cos, sin = _rope_table(inv, seq, tm)
    dmask, qdec, kdec, cdec = _retention_tables()

    h = x.reshape(t, d)
    xn = _rmsnorm(h, norm_mix_w[0], BF16, tm)
    out = None
    for l in range(depth):
        w = w_in[l].astype(BF16)
        w_gates = w[:, OFF_GATES:]
        w_dt = jnp.pad(w[:, OFF_DT:OFF_GATES], ((0, 0), (0, LANES - SSD_HEADS)))

        qk = _qk_proj(xn, w, cos, sin, seq, tm, tn)
        v, xc = _vxbc_proj(xn, w, w[:, OFF_XBC:OFF_DT], conv_w[l], conv_b[l], seq, tm, 4)
        gz = _proj(xn, w, OFF_G, RET_V + SSD_INNER, "silu", BF16, tm, tn, "gz_proj")
        gates = _proj(xn, w_gates, 0, 2 * D_MODEL, "sigmoid", BF16, tm, tn, "gate_proj")
        dt = _proj(xn, w_dt, 0, LANES, "none", F32, tm, LANES, "dt_proj")

        yr = _retention(qk, v, gz, dmask, qdec, kdec, cdec, batch, seq)
        ys = _ssd(xc, gz, dt, dt_bias[l], a_log[l], d_skip[l], ssd_norm_w[l], batch, seq)
        h = _merge(yr, ys, gates, h, ret_out[l].astype(BF16), ssd_out[l].astype(BF16), w_o[l].astype(BF16),
                   min(MERGE_TM, seq))

        last = l == depth - 1
        next_w = final_norm_w if last else norm_mix_w[l + 1]
        h, nxt = _ffn(h, norm_ffn_w[l], w_gate_up[l].astype(BF16), w_down[l].astype(BF16), next_w,
                      F32 if last else BF16, min(FFN_TM, seq))
        if last:
            out = nxt
        else:
            xn = nxt
    return out.reshape(batch, seq, d)
```

```python
import functools

import jax
import jax.numpy as jnp
from jax import lax
from jax.experimental import pallas as pl
from jax.experimental.pallas import tpu as pltpu

EPS = 1e-6
D_MODEL = 1024

RET_HEADS = 4
RET_DK = 256
RET_DV = 512
RET_QK = RET_HEADS * RET_DK
RET_V = RET_HEADS * RET_DV
ROPE_BASE = 10000.0
ROPE_HALF = RET_DK // 2

SSD_INNER = 2 * D_MODEL
SSD_HEADDIM = 64
SSD_HEADS = SSD_INNER // SSD_HEADDIM
SSD_GROUPS = 4
SSD_HPG = SSD_HEADS // SSD_GROUPS
SSD_STATE = 128
SSD_CONV = 4
SSD_BC = SSD_GROUPS * SSD_STATE
SSD_CONV_CH = SSD_INNER + 2 * SSD_BC

D_FF = 2816

OFF_V = 2 * RET_QK
OFF_G = OFF_V + RET_V
OFF_XBC = OFF_G + RET_V + SSD_INNER
OFF_DT = OFF_XBC + SSD_CONV_CH
OFF_GATES = OFF_DT + SSD_HEADS
D_IN = OFF_GATES + 2 * D_MODEL

LANES = 128
SUBLANES = 8
MXU_COLS = 256
SSD_CHUNK = 128
RET_CHUNK = 256
RET_BLOCK = 512
HEADS_PER_VREG = LANES // SSD_HEADDIM
SSD_PAIRS = SSD_HEADS // HEADS_PER_VREG
PAIRS_PER_GROUP = SSD_HPG // HEADS_PER_VREG

PROJ_TM = 1024
GZ_TN = 4096
VXBC_SPLIT = 2
MERGE_TM = 512
FFN_TM = 512

VMEM_LIMIT_BYTES = 56 * 1024 * 1024

LOG2_E = 1.4426950408889634

BF16 = jnp.bfloat16
F32 = jnp.float32


def _params(*semantics):
    return pltpu.CompilerParams(dimension_semantics=semantics, vmem_limit_bytes=VMEM_LIMIT_BYTES)


def _resident(shape, layer=None):
    if layer is None:
        return pl.BlockSpec(shape, lambda *_: (0,) * len(shape), pipeline_mode=pl.Buffered(1))
    return pl.BlockSpec((None,) + shape, lambda *_: (layer,) + (0,) * len(shape), pipeline_mode=pl.Buffered(1))


def _sigmoid(x):
    return 0.5 + 0.5 * jnp.tanh(0.5 * x)


def _silu(x):
    h = 0.5 * x
    return h + h * jnp.tanh(h)


def _softplus(x):
    return jnp.maximum(x, 0.0) + jnp.log(1.0 + jnp.exp(-jnp.abs(x)))


def _rms(x, w):
    return x * lax.rsqrt(jnp.mean(x * x, axis=-1, keepdims=True) + EPS) * w


_ACTIVATIONS = {"none": lambda a: a, "silu": _silu, "sigmoid": _sigmoid}


def _rmsnorm_kernel(x_ref, w_ref, o_ref):
    o_ref[...] = _rms(x_ref[...], w_ref[...]).astype(o_ref.dtype)


def _rmsnorm(x, w, out_dtype, tm):
    t, d = x.shape
    return pl.pallas_call(
        _rmsnorm_kernel,
        grid=(t // tm,),
        in_specs=[pl.BlockSpec((tm, d), lambda i: (i, 0)), pl.BlockSpec((1, d), lambda i: (0, 0))],
        out_specs=pl.BlockSpec((tm, d), lambda i: (i, 0)),
        out_shape=jax.ShapeDtypeStruct((t, d), out_dtype),
        compiler_params=_params("parallel"),
        name="rmsnorm",
    )(x, w.reshape(1, d))


def _rope_kernel(inv_ref, cos_ref, sin_ref, *, tm):
    pos = (pl.program_id(0) * tm + lax.broadcasted_iota(jnp.int32, (tm, ROPE_HALF), 0)).astype(F32)
    ang = pos * inv_ref[...]
    cos_ref[...] = jnp.cos(ang)
    sin_ref[...] = jnp.sin(ang)


def _rope_table(inv, seq, tm):
    spec = pl.BlockSpec((tm, ROPE_HALF), lambda i: (i, 0))
    shape = jax.ShapeDtypeStruct((seq, ROPE_HALF), F32)
    return pl.pallas_call(
        functools.partial(_rope_kernel, tm=tm),
        grid=(seq // tm,),
        in_specs=[pl.BlockSpec((1, ROPE_HALF), lambda i: (0, 0))],
        out_specs=[spec, spec],
        out_shape=[shape, shape],
        compiler_params=_params("parallel"),
        name="rope_table",
    )(inv)


def _proj_kernel(x_ref, w_ref, o_ref, *, act):
    acc = jnp.dot(x_ref[...], w_ref[...], preferred_element_type=F32)
    o_ref[...] = _ACTIVATIONS[act](acc).astype(o_ref.dtype)


def _proj(xn, w, layer, col0, ncols, act, out_dtype, tm, tn, name):
    t, d = xn.shape
    blk0 = col0 // tn
    return pl.pallas_call(
        functools.partial(_proj_kernel, act=act),
        grid=(t // tm, ncols // tn),
        in_specs=[pl.BlockSpec((tm, d), lambda i, j: (i, 0)),
                  pl.BlockSpec((None, d, tn), lambda i, j: (layer, 0, blk0 + j))],
        out_specs=pl.BlockSpec((tm, tn), lambda i, j: (i, j)),
        out_shape=jax.ShapeDtypeStruct((t, ncols), out_dtype),
        compiler_params=_params("parallel", "arbitrary"),
        name=name,
    )(xn, w)


def _qk_kernel(x_ref, w_ref, cos_ref, sin_ref, o_ref):
    acc = jnp.dot(x_ref[...], w_ref[...], preferred_element_type=F32)
    cos = cos_ref[...]
    sin = sin_ref[...]
    cos_k = cos * (RET_DK ** -0.5)
    sin_k = sin * (RET_DK ** -0.5)
    for h in range(2 * RET_HEADS):
        c, s = (cos, sin) if h < RET_HEADS else (cos_k, sin_k)
        lo = h * RET_DK
        t1 = acc[:, lo:lo + ROPE_HALF]
        t2 = acc[:, lo + ROPE_HALF:lo + RET_DK]
        o_ref[:, lo:lo + ROPE_HALF] = (t1 * c - t2 * s).astype(o_ref.dtype)
        o_ref[:, lo + ROPE_HALF:lo + RET_DK] = (t1 * s + t2 * c).astype(o_ref.dtype)


def _qk_proj(xn, w, layer, cos, sin, seq, tm):
    t, d = xn.shape
    npos = seq // tm
    n = 2 * RET_QK
    return pl.pallas_call(
        _qk_kernel,
        grid=(t // tm,),
        in_specs=[pl.BlockSpec((tm, d), lambda i: (i, 0)),
                  pl.BlockSpec((None, d, n), lambda i: (layer, 0, 0)),
                  pl.BlockSpec((tm, ROPE_HALF), lambda i: (i % npos, 0)),
                  pl.BlockSpec((tm, ROPE_HALF), lambda i: (i % npos, 0))],
        out_specs=pl.BlockSpec((tm, n), lambda i: (i, 0)),
        out_shape=jax.ShapeDtypeStruct((t, n), BF16),
        compiler_params=_params("parallel"),
        name="qk_proj_rotary",
    )(xn, w, cos, sin)


def _shift_rows(a, prev, back):
    n, width = a.shape
    rot = pltpu.roll(a.reshape(n // SUBLANES, SUBLANES, width), back, 1)
    below = jnp.concatenate([pltpu.roll(prev, back, 0)[None], rot[:-1]], axis=0)
    sub = lax.broadcasted_iota(jnp.int32, rot.shape, 1)
    return jnp.where(sub < back, below, rot).reshape(n, width)


def _vxbc_kernel(x_ref, wv_ref, wx_ref, cw_ref, cb_ref, ov_ref, ox_ref, carry_ref, *, tm, seq):
    i = pl.program_id(0)
    j = pl.program_id(1)

    @pl.when(i == 0)
    def _():
        carry_ref[j] = jnp.zeros(carry_ref.shape[1:], F32)

    x = x_ref[...]
    batch_start = lax.rem(i, seq // tm) == 0
    nv = ov_ref.shape[1] // MXU_COLS
    for s in range(ox_ref.shape[1] // MXU_COLS):
        cols = slice(s * MXU_COLS, (s + 1) * MXU_COLS)
        acc = jnp.dot(x, wx_ref[:, cols], preferred_element_type=F32)
        if s < nv:
            ov_ref[:, cols] = jnp.dot(x, wv_ref[:, cols], preferred_element_type=F32).astype(ov_ref.dtype)
        prev = jnp.where(batch_start, 0.0, carry_ref[j, :, cols])
        carry_ref[j, :, cols] = acc[tm - SUBLANES:, :]
        w0, w1, w2, w3 = (cw_ref[k:k + 1, cols] for k in range(SSD_CONV))
        back1 = _shift_rows(acc, prev, 1)
        pair_now = cb_ref[:, cols] + acc * w3 + back1 * w2
        pair_old = acc * w1 + back1 * w0
        pair_old_prev = prev * w1 + pltpu.roll(prev, 1, 0) * w0
        conv = pair_now + _shift_rows(pair_old, pair_old_prev, 2)
        ox_ref[:, cols] = _silu(conv).astype(ox_ref.dtype)


def _vxbc_proj(xn, w, w_xbc, layer, conv_w, conv_b, seq, tm, nsplit):
    t, d = xn.shape
    tnv = RET_V // nsplit
    tnx = SSD_CONV_CH // nsplit
    vblk0 = OFF_V // tnv
    return pl.pallas_call(
        functools.partial(_vxbc_kernel, tm=tm, seq=seq),
        grid=(t // tm, nsplit),
        in_specs=[pl.BlockSpec((tm, d), lambda i, j: (i, 0)),
                  pl.BlockSpec((None, d, tnv), lambda i, j: (layer, 0, vblk0 + j)),
                  pl.BlockSpec((None, d, tnx), lambda i, j: (layer, 0, j)),
                  pl.BlockSpec((None, SSD_CONV, tnx), lambda i, j: (layer, 0, j)),
                  pl.BlockSpec((None, 1, tnx), lambda i, j: (layer, 0, j))],
        out_specs=[pl.BlockSpec((tm, tnv), lambda i, j: (i, j)), pl.BlockSpec((tm, tnx), lambda i, j: (i, j))],
        out_shape=[jax.ShapeDtypeStruct((t, RET_V), BF16), jax.ShapeDtypeStruct((t, SSD_CONV_CH), BF16)],
        scratch_shapes=[pltpu.VMEM((nsplit, SUBLANES, tnx), F32)],
        compiler_params=_params("arbitrary", "arbitrary"),
        name="v_xbc_proj_conv",
    )(xn, w, w_xbc, conv_w, conv_b.reshape(conv_b.shape[0], 1, -1))


def _retention_kernel(q_ref, k_ref, v_ref, g_ref, dmask_ref, qdec_ref, kdec_ref, cdec_ref, o_ref, state_ref):
    @pl.when(pl.program_id(1) == 0)
    def _():
        state_ref[...] = jnp.zeros_like(state_ref)

    contract_last = (((1,), (1,)), ((), ()))
    contract_first = (((0,), (0,)), ((), ()))
    for c in range(RET_BLOCK // RET_CHUNK):
        rows = pl.ds(c * RET_CHUNK, RET_CHUNK)
        for h in range(RET_HEADS):
            q = q_ref[rows, h * RET_DK:(h + 1) * RET_DK]
            k = k_ref[rows, h * RET_DK:(h + 1) * RET_DK]
            v = v_ref[rows, h * RET_DV:(h + 1) * RET_DV]
            s = lax.dot_general(q, k, contract_last, preferred_element_type=F32) * dmask_ref[h]
            st = state_ref[h]
            y = jnp.dot(s.astype(BF16), v, preferred_element_type=F32)
            y = y + jnp.dot(q, st.astype(BF16), preferred_element_type=F32) * qdec_ref[h]
            kd = (k.astype(F32) * kdec_ref[h]).astype(BF16)
            state_ref[h] = st * cdec_ref[h] + lax.dot_general(kd, v, contract_first, preferred_element_type=F32)
            yn = y * lax.rsqrt(jnp.mean(y * y, axis=-1, keepdims=True) + EPS)
            gate = g_ref[rows, h * RET_DV:(h + 1) * RET_DV].astype(F32)
            o_ref[rows, h * RET_DV:(h + 1) * RET_DV] = (gate * yn).astype(o_ref.dtype)


def _retention(qk, v, gz, dmask, qdec, kdec, cdec, batch, seq):
    t = batch * seq
    nblk = seq // RET_BLOCK
    row = lambda b, i: b * nblk + i
    return pl.pallas_call(
        _retention_kernel,
        grid=(batch, nblk),
        in_specs=[pl.BlockSpec((RET_BLOCK, RET_QK), lambda b, i: (row(b, i), 0)),
                  pl.BlockSpec((RET_BLOCK, RET_QK), lambda b, i: (row(b, i), 1)),
                  pl.BlockSpec((RET_BLOCK, RET_V), lambda b, i: (row(b, i), 0)),
                  pl.BlockSpec((RET_BLOCK, RET_V), lambda b, i: (row(b, i), 0)),
                  _resident((RET_HEADS, RET_CHUNK, RET_CHUNK)),
                  _resident((RET_HEADS, RET_CHUNK, 1)),
                  _resident((RET_HEADS, RET_CHUNK, 1)),
                  _resident((RET_HEADS, 1, 1))],
        out_specs=pl.BlockSpec((RET_BLOCK, RET_V), lambda b, i: (row(b, i), 0)),
        out_shape=jax.ShapeDtypeStruct((t, RET_V), BF16),
        scratch_shapes=[pltpu.VMEM((RET_HEADS, RET_DK, RET_DV), F32)],
        compiler_params=_params("parallel", "arbitrary"),
        name="retention_scan",
    )(qk, qk, v, gz, dmask, qdec, kdec, cdec)


def _retention_tables():
    log_g = jnp.log(1.0 - 2.0 ** (-5.0 - jnp.arange(RET_HEADS, dtype=F32)))
    idx = jnp.arange(RET_CHUNK, dtype=F32)
    diff = idx[:, None] - idx[None, :]
    dmask = jnp.where((diff >= 0)[None], jnp.exp(log_g[:, None, None] * jnp.maximum(diff, 0.0)[None]), 0.0)
    qdec = jnp.exp(log_g[:, None] * (idx + 1.0))[:, :, None]
    kdec = jnp.exp(log_g[:, None] * (RET_CHUNK - 1.0 - idx))[:, :, None]
    cdec = jnp.exp(log_g * RET_CHUNK)[:, None, None]
    return dmask, qdec, kdec, cdec


def _cumsum_rows(x):
    n = x.shape[0]
    row = lax.broadcasted_iota(jnp.int32, x.shape, 0)
    shift = 1
    while shift < n:
        x = x + jnp.where(row >= shift, pltpu.roll(x, shift, 0), 0.0)
        shift *= 2
    return x


def _ssd_kernel(xc_ref, sz_ref, dt_ref, dtb_ref, alog_ref, dskip_ref, nw_ref, o_ref, state_ref, y_ref):
    c = SSD_CHUNK

    @pl.when(pl.program_id(1) == 0)
    def _():
        state_ref[...] = jnp.zeros_like(state_ref)

    dtp = _softplus(dt_ref[...] + dtb_ref[...])
    la = dtp * (-jnp.exp(alog_ref[...]))
    acs = _cumsum_rows(la) * LOG2_E
    acs_t = acs.T
    adj_t = acs_t - jnp.log2(dtp.T)

    row = lax.broadcasted_iota(jnp.int32, (c, c), 0)
    col = lax.broadcasted_iota(jnp.int32, (c, c), 1)
    causal = row >= col
    left = lax.broadcasted_iota(jnp.int32, (1, LANES), 1) < SSD_HEADDIM
    contract_last = (((1,), (1,)), ((), ()))

    for g in range(SSD_GROUPS):
        b_bf = xc_ref[:, SSD_INNER + g * SSD_STATE:SSD_INNER + (g + 1) * SSD_STATE]
        c_bf = xc_ref[:, SSD_INNER + SSD_BC + g * SSD_STATE:SSD_INNER + SSD_BC + (g + 1) * SSD_STATE]
        cb_g = lax.dot_general(c_bf, b_bf, contract_last, preferred_element_type=F32)
        b_t = b_bf.astype(F32).T
        c_f = c_bf.astype(F32)
        for pp in range(PAIRS_PER_GROUP):
            p = g * PAIRS_PER_GROUP + pp
            lhs_y, lhs_s, decs = [], [], []
            for r in (HEADS_PER_VREG * p, HEADS_PER_VREG * p + 1):
                a_i = jnp.broadcast_to(acs[:, r:r + 1], (c, c))
                adj_j = adj_t[r:r + 1, :]
                scores = cb_g * jnp.exp2(jnp.where(causal, a_i - adj_j, -1e30))
                lhs_y.append(scores.astype(BF16))
                lhs_y.append((c_f * jnp.exp2(a_i)).astype(BF16))
                a_last = acs_t[r:r + 1, c - 1:c]
                lhs_s.append((b_t * jnp.exp2(a_last - adj_j)).astype(BF16))
                decs.append(jnp.exp2(a_last))
            xs_p = xc_ref[:, p * LANES:(p + 1) * LANES]
            zero_x = jnp.zeros_like(xs_p)
            xs_l = jnp.where(left, xs_p, zero_x)
            xs_r = jnp.where(left, zero_x, xs_p)
            st = state_ref[p]
            st_bf = st.astype(BF16)
            zero_s = jnp.zeros_like(st_bf)
            st_l = jnp.where(left, st_bf, zero_s)
            st_r = jnp.where(left, zero_s, st_bf)
            y = jnp.dot(jnp.concatenate(lhs_y, axis=1), jnp.concatenate([xs_l, st_l, xs_r, st_r], axis=0),
                        preferred_element_type=F32)
            upd = jnp.dot(jnp.concatenate(lhs_s, axis=1), jnp.concatenate([xs_l, xs_r], axis=0),
                          preferred_element_type=F32)
            dec = jnp.where(left, decs[0], decs[1])
            state_ref[p] = st * dec + upd
            y_ref[:, p * LANES:(p + 1) * LANES] = y

    gw = SSD_INNER // SSD_GROUPS
    for g in range(SSD_GROUPS):
        cols = slice(g * gw, (g + 1) * gw)
        y = y_ref[:, cols] + dskip_ref[:, cols] * xc_ref[:, cols].astype(F32)
        y = y * sz_ref[:, cols].astype(F32)
        o_ref[:, cols] = _rms(y, nw_ref[:, cols]).astype(o_ref.dtype)


def _ssd(xc, gz, dt, dt_bias, a_log, d_skip, norm_w, batch, seq):
    t = batch * seq
    c = SSD_CHUNK
    nblk = seq // c
    row = lambda b, i: b * nblk + i
    pad = LANES - SSD_HEADS
    return pl.pallas_call(
        _ssd_kernel,
        grid=(batch, nblk),
        in_specs=[pl.BlockSpec((c, SSD_CONV_CH), lambda b, i: (row(b, i), 0)),
                  pl.BlockSpec((c, SSD_INNER), lambda b, i: (row(b, i), 1)),
                  pl.BlockSpec((c, LANES), lambda b, i: (row(b, i), 0)),
                  _resident((1, LANES)),
                  _resident((1, LANES)),
                  _resident((1, SSD_INNER)),
                  _resident((1, SSD_INNER))],
        out_specs=pl.BlockSpec((c, SSD_INNER), lambda b, i: (row(b, i), 0)),
        out_shape=jax.ShapeDtypeStruct((t, SSD_INNER), BF16),
        scratch_shapes=[pltpu.VMEM((SSD_PAIRS, SSD_STATE, LANES), F32),
                        pltpu.VMEM((c, SSD_INNER), F32)],
        compiler_params=_params("parallel", "arbitrary"),
        name="ssd_scan",
    )(xc, gz, dt,
      jnp.pad(dt_bias, (0, pad)).reshape(1, LANES), jnp.pad(a_log, (0, pad)).reshape(1, LANES),
      jnp.repeat(d_skip, SSD_HEADDIM).reshape(1, SSD_INNER), norm_w.reshape(1, SSD_INNER))


def _merge_kernel(yr_ref, ys_ref, gr_ref, gs_ref, h_ref, wr_ref, ws_ref, wo_ref, o_ref):
    o_ret = jnp.dot(yr_ref[...], wr_ref[...], preferred_element_type=F32)
    o_ssd = jnp.dot(ys_ref[...], ws_ref[...], preferred_element_type=F32)
    merged = gr_ref[...].astype(F32) * o_ret + gs_ref[...].astype(F32) * o_ssd
    o_ref[...] = h_ref[...] + jnp.dot(merged.astype(BF16), wo_ref[...], preferred_element_type=F32)


def _merge(yr, ys, gates, h, w_ret, w_ssd, w_o, layer, tm):
    t, d = h.shape
    rows = lambda i: (i, 0)
    return pl.pallas_call(
        _merge_kernel,
        grid=(t // tm,),
        in_specs=[pl.BlockSpec((tm, RET_V), rows),
                  pl.BlockSpec((tm, SSD_INNER), rows),
                  pl.BlockSpec((tm, d), lambda i: (i, 0)),
                  pl.BlockSpec((tm, d), lambda i: (i, 1)),
                  pl.BlockSpec((tm, d), rows),
                  _resident((RET_V, d), layer),
                  _resident((SSD_INNER, d), layer),
                  _resident((d, d), layer)],
        out_specs=pl.BlockSpec((tm, d), rows),
        out_shape=jax.ShapeDtypeStruct((t, d), F32),
        compiler_params=_params("parallel"),
        name="out_proj_merge",
    )(yr, ys, gates, gates, h, w_ret, w_ssd, w_o)


def _ffn_kernel(h_ref, nw_ref, wgu_ref, wd_ref, nw2_ref, o_ref, on_ref):
    x = h_ref[...]
    xn = _rms(x, nw_ref[...]).astype(BF16)
    gate = jnp.dot(xn, wgu_ref[:, :D_FF], preferred_element_type=F32)
    up = jnp.dot(xn, wgu_ref[:, D_FF:], preferred_element_type=F32)
    out = x + jnp.dot((_silu(gate) * up).astype(BF16), wd_ref[...], preferred_element_type=F32)
    o_ref[...] = out
    on_ref[...] = _rms(out, nw2_ref[...]).astype(on_ref.dtype)


def _ffn(h, norm_w, w_gate_up, w_down, layer, next_norm_w, next_dtype, tm):
    t, d = h.shape
    rows = lambda i: (i, 0)
    return pl.pallas_call(
        _ffn_kernel,
        grid=(t // tm,),
        in_specs=[pl.BlockSpec((tm, d), rows),
                  _resident((1, d)),
                  _resident((d, 2 * D_FF), layer),
                  _resident((D_FF, d), layer),
                  _resident((1, d))],
        out_specs=[pl.BlockSpec((tm, d), rows), pl.BlockSpec((tm, d), rows)],
        out_shape=[jax.ShapeDtypeStruct((t, d), F32), jax.ShapeDtypeStruct((t, d), next_dtype)],
        compiler_params=_params("parallel"),
        name="swiglu_ffn",
    )(h, norm_w.reshape(1, d), w_gate_up, w_down, next_norm_w.reshape(1, d))


def kernel(x, norm_mix_w, w_in, ret_out, conv_w, conv_b, dt_bias, a_log, d_skip, ssd_norm_w, ssd_out, w_o,
           norm_ffn_w, w_gate_up, w_down, final_norm_w):
    batch, seq, d = x.shape
    depth = w_in.shape[0]
    assert d == D_MODEL and w_in.shape[2] == D_IN
    assert seq % RET_BLOCK == 0 and seq % SSD_CHUNK == 0
    t = batch * seq
    tm = min(PROJ_TM, seq)

    inv = (ROPE_BASE ** (-jnp.arange(ROPE_HALF, dtype=F32) / ROPE_HALF)).reshape(1, ROPE_HALF)
    cos, sin = _rope_table(inv, seq, tm)
    dmask, qdec, kdec, cdec = _retention_tables()

    w = w_in.astype(BF16)
    w_xbc = w[:, :, OFF_XBC:OFF_DT]
    w_gates = w[:, :, OFF_GATES:]
    w_dt = jnp.pad(w[:, :, OFF_DT:OFF_GATES], ((0, 0), (0, 0), (0, LANES - SSD_HEADS)))
    w_ret, w_ssd, w_out = ret_out.astype(BF16), ssd_out.astype(BF16), w_o.astype(BF16)
    w_gu, w_dn = w_gate_up.astype(BF16), w_down.astype(BF16)

    h = x.reshape(t, d)
    xn = _rmsnorm(h, norm_mix_w[0], BF16, tm)
    out = None
    for l in range(depth):
        qk = _qk_proj(xn, w, l, cos, sin, seq, tm)
        v, xc = _vxbc_proj(xn, w, w_xbc, l, conv_w, conv_b, seq, tm, VXBC_SPLIT)
        gz = _proj(xn, w, l, OFF_G, RET_V + SSD_INNER, "silu", BF16, tm, GZ_TN, "gz_proj")
        gates = _proj(xn, w_gates, l, 0, 2 * D_MODEL, "sigmoid", BF16, tm, 2 * D_MODEL, "gate_proj")
        dt = _proj(xn, w_dt, l, 0, LANES, "none", F32, tm, LANES, "dt_proj")

        yr = _retention(qk, v, gz, dmask, qdec, kdec, cdec, batch, seq)
        ys = _ssd(xc, gz, dt, dt_bias[l], a_log[l], d_skip[l], ssd_norm_w[l], batch, seq)
        h = _merge(yr, ys, gates, h, w_ret, w_ssd, w_out, l, min(MERGE_TM, seq))

        last = l == depth - 1
        next_w = final_norm_w if last else norm_mix_w[l + 1]
        h, nxt = _ffn(h, norm_ffn_w[l], w_gu, w_dn, l, next_w, F32 if last else BF16, min(FFN_TM, seq))
        if last:
            out = nxt
        else:
            xn = nxt
    return out.reshape(batch, seq, d)
```

```python
import functools

import jax
import jax.numpy as jnp
from jax import lax
from jax.experimental import pallas as pl
from jax.experimental.pallas import tpu as pltpu

EPS = 1e-6
D_MODEL = 1024

RET_HEADS = 4
RET_DK = 256
RET_DV = 512
RET_QK = RET_HEADS * RET_DK
RET_V = RET_HEADS * RET_DV
ROPE_BASE = 10000.0
ROPE_HALF = RET_DK // 2

SSD_INNER = 2 * D_MODEL
SSD_HEADDIM = 64
SSD_HEADS = SSD_INNER // SSD_HEADDIM
SSD_GROUPS = 4
SSD_HPG = SSD_HEADS // SSD_GROUPS
SSD_STATE = 128
SSD_CONV = 4
SSD_BC = SSD_GROUPS * SSD_STATE
SSD_CONV_CH = SSD_INNER + 2 * SSD_BC

D_FF = 2816

OFF_V = 2 * RET_QK
OFF_G = OFF_V + RET_V
OFF_XBC = OFF_G + RET_V + SSD_INNER
OFF_DT = OFF_XBC + SSD_CONV_CH
OFF_GATES = OFF_DT + SSD_HEADS
D_IN = OFF_GATES + 2 * D_MODEL

LANES = 128
SUBLANES = 8
MXU_COLS = 256
SSD_CHUNK = 128
SSD_BLOCK = 512
RET_CHUNK = 256
RET_BLOCK = 512
HEADS_PER_VREG = LANES // SSD_HEADDIM
SSD_PAIRS = SSD_HEADS // HEADS_PER_VREG
PAIRS_PER_GROUP = SSD_HPG // HEADS_PER_VREG

PROJ_TM = 1024
GZ_TN = 4096
VXBC_SPLIT = 2
MERGE_TM = 512
FFN_TM = 512

VMEM_LIMIT_BYTES = 56 * 1024 * 1024

LOG2_E = 1.4426950408889634

BF16 = jnp.bfloat16
F32 = jnp.float32


def _params(*semantics):
    return pltpu.CompilerParams(dimension_semantics=semantics, vmem_limit_bytes=VMEM_LIMIT_BYTES)


def _resident(shape, layer=None):
    if layer is None:
        return pl.BlockSpec(shape, lambda *_: (0,) * len(shape), pipeline_mode=pl.Buffered(1))
    return pl.BlockSpec((None,) + shape, lambda *_: (layer,) + (0,) * len(shape), pipeline_mode=pl.Buffered(1))


def _sigmoid(x):
    return 0.5 + 0.5 * jnp.tanh(0.5 * x)


def _silu(x):
    h = 0.5 * x
    return h + h * jnp.tanh(h)


def _softplus(x):
    return jnp.maximum(x, 0.0) + jnp.log(1.0 + jnp.exp(-jnp.abs(x)))


def _rms(x, w):
    return x * lax.rsqrt(jnp.mean(x * x, axis=-1, keepdims=True) + EPS) * w


_ACTIVATIONS = {"none": lambda a: a, "silu": _silu, "sigmoid": _sigmoid}


def _rmsnorm_kernel(x_ref, w_ref, o_ref):
    o_ref[...] = _rms(x_ref[...], w_ref[...]).astype(o_ref.dtype)


def _rmsnorm(x, w, out_dtype, tm):
    t, d = x.shape
    return pl.pallas_call(
        _rmsnorm_kernel,
        grid=(t // tm,),
        in_specs=[pl.BlockSpec((tm, d), lambda i: (i, 0)), pl.BlockSpec((1, d), lambda i: (0, 0))],
        out_specs=pl.BlockSpec((tm, d), lambda i: (i, 0)),
        out_shape=jax.ShapeDtypeStruct((t, d), out_dtype),
        compiler_params=_params("parallel"),
        name="rmsnorm",
    )(x, w.reshape(1, d))


def _rope_kernel(inv_ref, cos_ref, sin_ref, *, tm):
    pos = (pl.program_id(0) * tm + lax.broadcasted_iota(jnp.int32, (tm, ROPE_HALF), 0)).astype(F32)
    ang = pos * inv_ref[...]
    cos_ref[...] = jnp.cos(ang)
    sin_ref[...] = jnp.sin(ang)


def _rope_table(inv, seq, tm):
    spec = pl.BlockSpec((tm, ROPE_HALF), lambda i: (i, 0))
    shape = jax.ShapeDtypeStruct((seq, ROPE_HALF), F32)
    return pl.pallas_call(
        functools.partial(_rope_kernel, tm=tm),
        grid=(seq // tm,),
        in_specs=[pl.BlockSpec((1, ROPE_HALF), lambda i: (0, 0))],
        out_specs=[spec, spec],
        out_shape=[shape, shape],
        compiler_params=_params("parallel"),
        name="rope_table",
    )(inv)


def _proj_kernel(x_ref, w_ref, o_ref, *, act):
    acc = jnp.dot(x_ref[...], w_ref[...], preferred_element_type=F32)
    o_ref[...] = _ACTIVATIONS[act](acc).astype(o_ref.dtype)


def _proj(xn, w, layer, col0, ncols, act, out_dtype, tm, tn, name):
    t, d = xn.shape
    blk0 = col0 // tn
    return pl.pallas_call(
        functools.partial(_proj_kernel, act=act),
        grid=(t // tm, ncols // tn),
        in_specs=[pl.BlockSpec((tm, d), lambda i, j: (i, 0)),
                  pl.BlockSpec((None, d, tn), lambda i, j: (layer, 0, blk0 + j))],
        out_specs=pl.BlockSpec((tm, tn), lambda i, j: (i, j)),
        out_shape=jax.ShapeDtypeStruct((t, ncols), out_dtype),
        compiler_params=_params("parallel", "arbitrary"),
        name=name,
    )(xn, w)


def _gates_dt_kernel(x_ref, w_ref, og_ref, odt_ref):
    acc = jnp.dot(x_ref[...], w_ref[...], preferred_element_type=F32)
    ng = og_ref.shape[1]
    og_ref[...] = _sigmoid(acc[:, :ng]).astype(og_ref.dtype)
    odt_ref[...] = acc[:, ng:]


def _gates_dt_proj(xn, w, layer, tm):
    t, d = xn.shape
    ng = 2 * D_MODEL
    n = w.shape[2]
    return pl.pallas_call(
        _gates_dt_kernel,
        grid=(t // tm,),
        in_specs=[pl.BlockSpec((tm, d), lambda i: (i, 0)), pl.BlockSpec((None, d, n), lambda i: (layer, 0, 0))],
        out_specs=[pl.BlockSpec((tm, ng), lambda i: (i, 0)), pl.BlockSpec((tm, n - ng), lambda i: (i, 0))],
        out_shape=[jax.ShapeDtypeStruct((t, ng), BF16), jax.ShapeDtypeStruct((t, n - ng), F32)],
        compiler_params=_params("parallel"),
        name="gate_dt_proj",
    )(xn, w)


def _qk_kernel(x_ref, w_ref, cos_ref, sin_ref, o_ref):
    acc = jnp.dot(x_ref[...], w_ref[...], preferred_element_type=F32)
    cos = cos_ref[...]
    sin = sin_ref[...]
    cos_k = cos * (RET_DK ** -0.5)
    sin_k = sin * (RET_DK ** -0.5)
    for h in range(2 * RET_HEADS):
        c, s = (cos, sin) if h < RET_HEADS else (cos_k, sin_k)
        lo = h * RET_DK
        t1 = acc[:, lo:lo + ROPE_HALF]
        t2 = acc[:, lo + ROPE_HALF:lo + RET_DK]
        o_ref[:, lo:lo + ROPE_HALF] = (t1 * c - t2 * s).astype(o_ref.dtype)
        o_ref[:, lo + ROPE_HALF:lo + RET_DK] = (t1 * s + t2 * c).astype(o_ref.dtype)


def _qk_proj(xn, w, layer, cos, sin, seq, tm):
    t, d = xn.shape
    npos = seq // tm
    n = 2 * RET_QK
    return pl.pallas_call(
        _qk_kernel,
        grid=(t // tm,),
        in_specs=[pl.BlockSpec((tm, d), lambda i: (i, 0)),
                  pl.BlockSpec((None, d, n), lambda i: (layer, 0, 0)),
                  pl.BlockSpec((tm, ROPE_HALF), lambda i: (i % npos, 0)),
                  pl.BlockSpec((tm, ROPE_HALF), lambda i: (i % npos, 0))],
        out_specs=pl.BlockSpec((tm, n), lambda i: (i, 0)),
        out_shape=jax.ShapeDtypeStruct((t, n), BF16),
        compiler_params=_params("parallel"),
        name="qk_proj_rotary",
    )(xn, w, cos, sin)


def _shift_rows(a, prev, back):
    n, width = a.shape
    rot = pltpu.roll(a.reshape(n // SUBLANES, SUBLANES, width), back, 1)
    below = jnp.concatenate([pltpu.roll(prev, back, 0)[None], rot[:-1]], axis=0)
    sub = lax.broadcasted_iota(jnp.int32, rot.shape, 1)
    return jnp.where(sub < back, below, rot).reshape(n, width)


def _vxbc_kernel(x_ref, wv_ref, wx_ref, cw_ref, cb_ref, ov_ref, ox_ref, carry_ref, *, tm, seq):
    i = pl.program_id(0)
    j = pl.program_id(1)

    @pl.when(i == 0)
    def _():
        carry_ref[j] = jnp.zeros(carry_ref.shape[1:], F32)

    x = x_ref[...]
    batch_start = lax.rem(i, seq // tm) == 0
    nv = ov_ref.shape[1] // MXU_COLS
    for s in range(ox_ref.shape[1] // MXU_COLS):
        cols = slice(s * MXU_COLS, (s + 1) * MXU_COLS)
        acc = jnp.dot(x, wx_ref[:, cols], preferred_element_type=F32)
        if s < nv:
            ov_ref[:, cols] = jnp.dot(x, wv_ref[:, cols], preferred_element_type=F32).astype(ov_ref.dtype)
        prev = jnp.where(batch_start, 0.0, carry_ref[j, :, cols])
        carry_ref[j, :, cols] = acc[tm - SUBLANES:, :]
        w0, w1, w2, w3 = (cw_ref[k:k + 1, cols] for k in range(SSD_CONV))
        back1 = _shift_rows(acc, prev, 1)
        pair_now = cb_ref[:, cols] + acc * w3 + back1 * w2
        pair_old = acc * w1 + back1 * w0
        pair_old_prev = prev * w1 + pltpu.roll(prev, 1, 0) * w0
        conv = pair_now + _shift_rows(pair_old, pair_old_prev, 2)
        ox_ref[:, cols] = _silu(conv).astype(ox_ref.dtype)


def _vxbc_proj(xn, w, w_xbc, layer, conv_w, conv_b, seq, tm, nsplit):
    t, d = xn.shape
    tnv = RET_V // nsplit
    tnx = SSD_CONV_CH // nsplit
    vblk0 = OFF_V // tnv
    return pl.pallas_call(
        functools.partial(_vxbc_kernel, tm=tm, seq=seq),
        grid=(t // tm, nsplit),
        in_specs=[pl.BlockSpec((tm, d), lambda i, j: (i, 0)),
                  pl.BlockSpec((None, d, tnv), lambda i, j: (layer, 0, vblk0 + j)),
                  pl.BlockSpec((None, d, tnx), lambda i, j: (layer, 0, j)),
                  pl.BlockSpec((None, SSD_CONV, tnx), lambda i, j: (layer, 0, j)),
                  pl.BlockSpec((None, 1, tnx), lambda i, j: (layer, 0, j))],
        out_specs=[pl.BlockSpec((tm, tnv), lambda i, j: (i, j)), pl.BlockSpec((tm, tnx), lambda i, j: (i, j))],
        out_shape=[jax.ShapeDtypeStruct((t, RET_V), BF16), jax.ShapeDtypeStruct((t, SSD_CONV_CH), BF16)],
        scratch_shapes=[pltpu.VMEM((nsplit, SUBLANES, tnx), F32)],
        compiler_params=_params("arbitrary", "arbitrary"),
        name="v_xbc_proj_conv",
    )(xn, w, w_xbc, conv_w, conv_b.reshape(conv_b.shape[0], 1, -1))


def _retention_kernel(q_ref, k_ref, v_ref, g_ref, dmask_ref, qdec_ref, kdec_ref, cdec_ref, o_ref, state_ref):
    @pl.when(pl.program_id(1) == 0)
    def _():
        state_ref[...] = jnp.zeros_like(state_ref)

    contract_last = (((1,), (1,)), ((), ()))
    contract_first = (((0,), (0,)), ((), ()))
    for c in range(RET_BLOCK // RET_CHUNK):
        rows = pl.ds(c * RET_CHUNK, RET_CHUNK)
        for h in range(RET_HEADS):
            q = q_ref[rows, h * RET_DK:(h + 1) * RET_DK]
            k = k_ref[rows, h * RET_DK:(h + 1) * RET_DK]
            v = v_ref[rows, h * RET_DV:(h + 1) * RET_DV]
            s = lax.dot_general(q, k, contract_last, preferred_element_type=F32) * dmask_ref[h]
            st = state_ref[h]
            lhs = jnp.concatenate([s.astype(BF16), q * qdec_ref[h].astype(BF16)], axis=1)
            rhs = jnp.concatenate([v, st.astype(BF16)], axis=0)
            y = jnp.dot(lhs, rhs, preferred_element_type=F32)
            kd = k * kdec_ref[h].astype(BF16)
            state_ref[h] = st * cdec_ref[h] + lax.dot_general(kd, v, contract_first, preferred_element_type=F32)
            yn = (y * lax.rsqrt(jnp.mean(y * y, axis=-1, keepdims=True) + EPS)).astype(BF16)
            o_ref[rows, h * RET_DV:(h + 1) * RET_DV] = g_ref[rows, h * RET_DV:(h + 1) * RET_DV] * yn


def _retention(qk, v, gz, dmask, qdec, kdec, cdec, batch, seq):
    t = batch * seq
    nblk = seq // RET_BLOCK
    row = lambda b, i: b * nblk + i
    return pl.pallas_call(
        _retention_kernel,
        grid=(batch, nblk),
        in_specs=[pl.BlockSpec((RET_BLOCK, RET_QK), lambda b, i: (row(b, i), 0)),
                  pl.BlockSpec((RET_BLOCK, RET_QK), lambda b, i: (row(b, i), 1)),
                  pl.BlockSpec((RET_BLOCK, RET_V), lambda b, i: (row(b, i), 0)),
                  pl.BlockSpec((RET_BLOCK, RET_V), lambda b, i: (row(b, i), 0)),
                  _resident((RET_HEADS, RET_CHUNK, RET_CHUNK)),
                  _resident((RET_HEADS, RET_CHUNK, 1)),
                  _resident((RET_HEADS, RET_CHUNK, 1)),
                  _resident((RET_HEADS, 1, 1))],
        out_specs=pl.BlockSpec((RET_BLOCK, RET_V), lambda b, i: (row(b, i), 0)),
        out_shape=jax.ShapeDtypeStruct((t, RET_V), BF16),
        scratch_shapes=[pltpu.VMEM((RET_HEADS, RET_DK, RET_DV), F32)],
        compiler_params=_params("parallel", "arbitrary"),
        name="retention_scan",
    )(qk, qk, v, gz, dmask, qdec, kdec, cdec)


def _retention_tables():
    log_g = jnp.log(1.0 - 2.0 ** (-5.0 - jnp.arange(RET_HEADS, dtype=F32)))
    idx = jnp.arange(RET_CHUNK, dtype=F32)
    diff = idx[:, None] - idx[None, :]
    dmask = jnp.where((diff >= 0)[None], jnp.exp(log_g[:, None, None] * jnp.maximum(diff, 0.0)[None]), 0.0)
    qdec = jnp.exp(log_g[:, None] * (idx + 1.0))[:, :, None]
    kdec = jnp.exp(log_g[:, None] * (RET_CHUNK - 1.0 - idx))[:, :, None]
    cdec = jnp.exp(log_g * RET_CHUNK)[:, None, None]
    return dmask, qdec, kdec, cdec


def _cumsum_rows(x):
    n = x.shape[0]
    row = lax.broadcasted_iota(jnp.int32, x.shape, 0)
    shift = 1
    while shift < n:
        x = x + jnp.where(row >= shift, pltpu.roll(x, shift, 0), 0.0)
        shift *= 2
    return x


def _ssd_kernel(xc_ref, dt_ref, dtb_ref, alog_ref, o_ref, state_ref):
    @pl.when(pl.program_id(1) == 0)
    def _():
        state_ref[...] = jnp.zeros_like(state_ref)

    for cc in range(SSD_BLOCK // SSD_CHUNK):
        rows = pl.ds(cc * SSD_CHUNK, SSD_CHUNK)
        _ssd_chunk(xc_ref.at[rows], dt_ref.at[rows], dtb_ref, alog_ref, o_ref.at[rows], state_ref)


def _ssd_chunk(xc_ref, dt_ref, dtb_ref, alog_ref, o_ref, state_ref):
    c = SSD_CHUNK
    dtp = _softplus(dt_ref[...] + dtb_ref[...])
    la = dtp * (-jnp.exp(alog_ref[...]))
    acs = _cumsum_rows(la) * LOG2_E
    acs_t = acs.T
    adj_t = acs_t - jnp.log2(dtp.T)

    row = lax.broadcasted_iota(jnp.int32, (c, c), 0)
    col = lax.broadcasted_iota(jnp.int32, (c, c), 1)
    causal = row >= col
    left = lax.broadcasted_iota(jnp.int32, (1, LANES), 1) < SSD_HEADDIM
    contract_last = (((1,), (1,)), ((), ()))

    for g in range(SSD_GROUPS):
        b_bf = xc_ref[:, SSD_INNER + g * SSD_STATE:SSD_INNER + (g + 1) * SSD_STATE]
        c_bf = xc_ref[:, SSD_INNER + SSD_BC + g * SSD_STATE:SSD_INNER + SSD_BC + (g + 1) * SSD_STATE]
        cb_g = lax.dot_general(c_bf, b_bf, contract_last, preferred_element_type=F32).astype(BF16)
        b_t = b_bf.astype(F32).T.astype(BF16)
        for pp in range(PAIRS_PER_GROUP):
            p = g * PAIRS_PER_GROUP + pp
            lhs_y, lhs_s, decs = [], [], []
            for r in (HEADS_PER_VREG * p, HEADS_PER_VREG * p + 1):
                a_i = jnp.broadcast_to(acs[:, r:r + 1], (c, c))
                adj_j = adj_t[r:r + 1, :]
                lhs_y.append(cb_g * jnp.exp2(jnp.where(causal, a_i - adj_j, -1e30)).astype(BF16))
                lhs_y.append(c_bf * jnp.exp2(a_i).astype(BF16))
                a_last = acs_t[r:r + 1, c - 1:c]
                lhs_s.append(b_t * jnp.exp2(a_last - adj_j).astype(BF16))
                decs.append(jnp.exp2(a_last))
            xs_p = xc_ref[:, p * LANES:(p + 1) * LANES]
            zero_x = jnp.zeros_like(xs_p)
            xs_l = jnp.where(left, xs_p, zero_x)
            xs_r = jnp.where(left, zero_x, xs_p)
            st = state_ref[p]
            st_bf = st.astype(BF16)
            zero_s = jnp.zeros_like(st_bf)
            st_l = jnp.where(left, st_bf, zero_s)
            st_r = jnp.where(left, zero_s, st_bf)
            y = jnp.dot(jnp.concatenate(lhs_y, axis=1), jnp.concatenate([xs_l, st_l, xs_r, st_r], axis=0),
                        preferred_element_type=F32)
            upd = jnp.dot(jnp.concatenate(lhs_s, axis=1), jnp.concatenate([xs_l, xs_r], axis=0),
                          preferred_element_type=F32)
            dec = jnp.where(left, decs[0], decs[1])
            state_ref[p] = st * dec + upd
            o_ref[:, p * LANES:(p + 1) * LANES] = y.astype(o_ref.dtype)


def _ssd(xc, dt, dt_bias, a_log, batch, seq):
    t = batch * seq
    c = SSD_BLOCK
    nblk = seq // c
    row = lambda b, i: b * nblk + i
    pad = LANES - SSD_HEADS
    return pl.pallas_call(
        _ssd_kernel,
        grid=(batch, nblk),
        in_specs=[pl.BlockSpec((c, SSD_CONV_CH), lambda b, i: (row(b, i), 0)),
                  pl.BlockSpec((c, LANES), lambda b, i: (row(b, i), 0)),
                  _resident((1, LANES)),
                  _resident((1, LANES))],
        out_specs=pl.BlockSpec((c, SSD_INNER), lambda b, i: (row(b, i), 0)),
        out_shape=jax.ShapeDtypeStruct((t, SSD_INNER), BF16),
        scratch_shapes=[pltpu.VMEM((SSD_PAIRS, SSD_STATE, LANES), F32)],
        compiler_params=_params("parallel", "arbitrary"),
        name="ssd_scan",
    )(xc, dt, jnp.pad(dt_bias, (0, pad)).reshape(1, LANES), jnp.pad(a_log, (0, pad)).reshape(1, LANES))


def _merge_kernel(yr_ref, y_ref, xs_ref, sz_ref, gr_ref, gs_ref, h_ref, dskip_ref, nw_ref, wr_ref, ws_ref, wo_ref,
                  o_ref):
    o_ret = jnp.dot(yr_ref[...], wr_ref[...], preferred_element_type=F32)
    gw = SSD_INNER // SSD_GROUPS
    o_ssd = None
    for g in range(SSD_GROUPS):
        cols = slice(g * gw, (g + 1) * gw)
        y = y_ref[:, cols].astype(F32) + dskip_ref[:, cols] * xs_ref[:, cols].astype(F32)
        y = y * sz_ref[:, cols].astype(F32)
        part = jnp.dot(_rms(y, nw_ref[:, cols]).astype(BF16), ws_ref[cols, :], preferred_element_type=F32)
        o_ssd = part if o_ssd is None else o_ssd + part
    merged = gr_ref[...].astype(F32) * o_ret + gs_ref[...].astype(F32) * o_ssd
    o_ref[...] = h_ref[...] + jnp.dot(merged.astype(BF16), wo_ref[...], preferred_element_type=F32)


def _merge(yr, y_ssd, xc, gz, gates, h, d_skip, norm_w, w_ret, w_ssd, w_o, layer, tm):
    t, d = h.shape
    rows = lambda i: (i, 0)
    return pl.pallas_call(
        _merge_kernel,
        grid=(t // tm,),
        in_specs=[pl.BlockSpec((tm, RET_V), rows),
                  pl.BlockSpec((tm, SSD_INNER), rows),
                  pl.BlockSpec((tm, SSD_INNER), rows),
                  pl.BlockSpec((tm, SSD_INNER), lambda i: (i, 1)),
                  pl.BlockSpec((tm, d), lambda i: (i, 0)),
                  pl.BlockSpec((tm, d), lambda i: (i, 1)),
                  pl.BlockSpec((tm, d), rows),
                  _resident((1, SSD_INNER)),
                  _resident((1, SSD_INNER)),
                  _resident((RET_V, d), layer),
                  _resident((SSD_INNER, d), layer),
                  _resident((d, d), layer)],
        out_specs=pl.BlockSpec((tm, d), rows),
        out_shape=jax.ShapeDtypeStruct((t, d), F32),
        compiler_params=_params("parallel"),
        name="out_proj_merge",
    )(yr, y_ssd, xc, gz, gates, gates, h,
      jnp.repeat(d_skip, SSD_HEADDIM).reshape(1, SSD_INNER), norm_w.reshape(1, SSD_INNER), w_ret, w_ssd, w_o)


def _ffn_kernel(h_ref, nw_ref, wgu_ref, wd_ref, nw2_ref, o_ref, on_ref):
    x = h_ref[...]
    xn = _rms(x, nw_ref[...]).astype(BF16)
    gate = jnp.dot(xn, wgu_ref[:, :D_FF], preferred_element_type=F32)
    up = jnp.dot(xn, wgu_ref[:, D_FF:], preferred_element_type=F32)
    out = x + jnp.dot((_silu(gate) * up).astype(BF16), wd_ref[...], preferred_element_type=F32)
    o_ref[...] = out
    on_ref[...] = _rms(out, nw2_ref[...]).astype(on_ref.dtype)


def _ffn(h, norm_w, w_gate_up, w_down, layer, next_norm_w, next_dtype, tm):
    t, d = h.shape
    rows = lambda i: (i, 0)
    return pl.pallas_call(
        _ffn_kernel,
        grid=(t // tm,),
        in_specs=[pl.BlockSpec((tm, d), rows),
                  _resident((1, d)),
                  _resident((d, 2 * D_FF), layer),
                  _resident((D_FF, d), layer),
                  _resident((1, d))],
        out_specs=[pl.BlockSpec((tm, d), rows), pl.BlockSpec((tm, d), rows)],
        out_shape=[jax.ShapeDtypeStruct((t, d), F32), jax.ShapeDtypeStruct((t, d), next_dtype)],
        compiler_params=_params("parallel"),
        name="swiglu_ffn",
    )(h, norm_w.reshape(1, d), w_gate_up, w_down, next_norm_w.reshape(1, d))


def kernel(x, norm_mix_w, w_in, ret_out, conv_w, conv_b, dt_bias, a_log, d_skip, ssd_norm_w, ssd_out, w_o,
           norm_ffn_w, w_gate_up, w_down, final_norm_w):
    batch, seq, d = x.shape
    depth = w_in.shape[0]
    assert d == D_MODEL and w_in.shape[2] == D_IN
    assert seq % RET_BLOCK == 0 and seq % SSD_BLOCK == 0
    t = batch * seq
    tm = min(PROJ_TM, seq)

    inv = (ROPE_BASE ** (-jnp.arange(ROPE_HALF, dtype=F32) / ROPE_HALF)).reshape(1, ROPE_HALF)
    cos, sin = _rope_table(inv, seq, tm)
    dmask, qdec, kdec, cdec = _retention_tables()

    w = w_in.astype(BF16)
    w_xbc = w[:, :, OFF_XBC:OFF_DT]
    w_gates_dt = jnp.concatenate(
        [w[:, :, OFF_GATES:], jnp.pad(w[:, :, OFF_DT:OFF_GATES], ((0, 0), (0, 0), (0, LANES - SSD_HEADS)))], axis=2)
    w_ret, w_ssd, w_out = ret_out.astype(BF16), ssd_out.astype(BF16), w_o.astype(BF16)
    w_gu, w_dn = w_gate_up.astype(BF16), w_down.astype(BF16)

    h = x.reshape(t, d)
    xn = _rmsnorm(h, norm_mix_w[0], BF16, tm)
    out = None
    for l in range(depth):
        qk = _qk_proj(xn, w, l, cos, sin, seq, tm)
        v, xc = _vxbc_proj(xn, w, w_xbc, l, conv_w, conv_b, seq, tm, VXBC_SPLIT)
        gz = _proj(xn, w, l, OFF_G, RET_V + SSD_INNER, "silu", BF16, tm, GZ_TN, "gz_proj")
        gates, dt = _gates_dt_proj(xn, w_gates_dt, l, tm)

        yr = _retention(qk, v, gz, dmask, qdec, kdec, cdec, batch, seq)
        y_ssd = _ssd(xc, dt, dt_bias[l], a_log[l], batch, seq)
        h = _merge(yr, y_ssd, xc, gz, gates, h, d_skip[l], ssd_norm_w[l], w_ret, w_ssd, w_out, l,
                   min(MERGE_TM, seq))

        last = l == depth - 1
        next_w = final_norm_w if last else norm_mix_w[l + 1]
        h, nxt = _ffn(h, norm_ffn_w[l], w_gu, w_dn, l, next_w, F32 if last else BF16, min(FFN_TM, seq))
        if last:
            out = nxt
        else:
            xn = nxt
    return out.reshape(batch, seq, d)
```

```python
import functools

import jax
import jax.numpy as jnp
from jax import lax
from jax.experimental import pallas as pl
from jax.experimental.pallas import tpu as pltpu

EPS = 1e-6
D_MODEL = 1024

RET_HEADS = 4
RET_DK = 256
RET_DV = 512
RET_QK = RET_HEADS * RET_DK
RET_V = RET_HEADS * RET_DV
ROPE_BASE = 10000.0
ROPE_HALF = RET_DK // 2

SSD_INNER = 2 * D_MODEL
SSD_HEADDIM = 64
SSD_HEADS = SSD_INNER // SSD_HEADDIM
SSD_GROUPS = 4
SSD_HPG = SSD_HEADS // SSD_GROUPS
SSD_STATE = 128
SSD_CONV = 4
SSD_BC = SSD_GROUPS * SSD_STATE
SSD_CONV_CH = SSD_INNER + 2 * SSD_BC

D_FF = 2816

OFF_V = 2 * RET_QK
OFF_G = OFF_V + RET_V
OFF_XBC = OFF_G + RET_V + SSD_INNER
OFF_DT = OFF_XBC + SSD_CONV_CH
OFF_GATES = OFF_DT + SSD_HEADS
D_IN = OFF_GATES + 2 * D_MODEL

LANES = 128
SUBLANES = 8
MXU_COLS = 256
SSD_CHUNK = 128
SSD_BLOCK = 512
RET_CHUNK = 256
RET_BLOCK = 1024
HEADS_PER_VREG = LANES // SSD_HEADDIM
SSD_PAIRS = SSD_HEADS // HEADS_PER_VREG
PAIRS_PER_GROUP = SSD_HPG // HEADS_PER_VREG

PROJ_TM = 1024
GZ_TN = 4096
VXBC_SPLIT = 1
MERGE_TM = 512
FFN_TM = 1024

VMEM_LIMIT_BYTES = 56 * 1024 * 1024

LOG2_E = 1.4426950408889634

BF16 = jnp.bfloat16
F32 = jnp.float32


def _params(*semantics):
    return pltpu.CompilerParams(dimension_semantics=semantics, vmem_limit_bytes=VMEM_LIMIT_BYTES)


def _resident(shape, layer=None):
    if layer is None:
        return pl.BlockSpec(shape, lambda *_: (0,) * len(shape), pipeline_mode=pl.Buffered(1))
    return pl.BlockSpec((None,) + shape, lambda *_: (layer,) + (0,) * len(shape), pipeline_mode=pl.Buffered(1))


def _sigmoid(x):
    return 0.5 + 0.5 * jnp.tanh(0.5 * x)


def _silu(x):
    h = 0.5 * x
    return h + h * jnp.tanh(h)


def _softplus(x):
    return jnp.maximum(x, 0.0) + jnp.log(1.0 + jnp.exp(-jnp.abs(x)))


def _rms(x, w):
    return x * lax.rsqrt(jnp.mean(x * x, axis=-1, keepdims=True) + EPS) * w


_ACTIVATIONS = {"none": lambda a: a, "silu": _silu, "sigmoid": _sigmoid}


def _rmsnorm_kernel(x_ref, w_ref, o_ref):
    o_ref[...] = _rms(x_ref[...], w_ref[...]).astype(o_ref.dtype)


def _rmsnorm(x, w, out_dtype, tm):
    t, d = x.shape
    return pl.pallas_call(
        _rmsnorm_kernel,
        grid=(t // tm,),
        in_specs=[pl.BlockSpec((tm, d), lambda i: (i, 0)), pl.BlockSpec((1, d), lambda i: (0, 0))],
        out_specs=pl.BlockSpec((tm, d), lambda i: (i, 0)),
        out_shape=jax.ShapeDtypeStruct((t, d), out_dtype),
        compiler_params=_params("parallel"),
        name="rmsnorm",
    )(x, w.reshape(1, d))


def _rope_kernel(inv_ref, cos_ref, sin_ref, *, tm):
    pos = (pl.program_id(0) * tm + lax.broadcasted_iota(jnp.int32, (tm, ROPE_HALF), 0)).astype(F32)
    ang = pos * inv_ref[...]
    cos_ref[...] = jnp.cos(ang)
    sin_ref[...] = jnp.sin(ang)


def _rope_table(inv, seq, tm):
    spec = pl.BlockSpec((tm, ROPE_HALF), lambda i: (i, 0))
    shape = jax.ShapeDtypeStruct((seq, ROPE_HALF), F32)
    return pl.pallas_call(
        functools.partial(_rope_kernel, tm=tm),
        grid=(seq // tm,),
        in_specs=[pl.BlockSpec((1, ROPE_HALF), lambda i: (0, 0))],
        out_specs=[spec, spec],
        out_shape=[shape, shape],
        compiler_params=_params("parallel"),
        name="rope_table",
    )(inv)


def _proj_kernel(x_ref, w_ref, o_ref, *, act):
    acc = jnp.dot(x_ref[...], w_ref[...], preferred_element_type=F32)
    o_ref[...] = _ACTIVATIONS[act](acc).astype(o_ref.dtype)


def _proj(xn, w, layer, col0, ncols, act, out_dtype, tm, tn, name):
    t, d = xn.shape
    blk0 = col0 // tn
    return pl.pallas_call(
        functools.partial(_proj_kernel, act=act),
        grid=(t // tm, ncols // tn),
        in_specs=[pl.BlockSpec((tm, d), lambda i, j: (i, 0)),
                  pl.BlockSpec((None, d, tn), lambda i, j: (layer, 0, blk0 + j))],
        out_specs=pl.BlockSpec((tm, tn), lambda i, j: (i, j)),
        out_shape=jax.ShapeDtypeStruct((t, ncols), out_dtype),
        compiler_params=_params("parallel", "arbitrary"),
        name=name,
    )(xn, w)


def _gates_dt_kernel(x_ref, w_ref, og_ref, odt_ref):
    acc = jnp.dot(x_ref[...], w_ref[...], preferred_element_type=F32)
    ng = og_ref.shape[1]
    og_ref[...] = _sigmoid(acc[:, :ng]).astype(og_ref.dtype)
    odt_ref[...] = acc[:, ng:]


def _gates_dt_proj(xn, w, layer, tm):
    t, d = xn.shape
    ng = 2 * D_MODEL
    n = w.shape[2]
    return pl.pallas_call(
        _gates_dt_kernel,
        grid=(t // tm,),
        in_specs=[pl.BlockSpec((tm, d), lambda i: (i, 0)), pl.BlockSpec((None, d, n), lambda i: (layer, 0, 0))],
        out_specs=[pl.BlockSpec((tm, ng), lambda i: (i, 0)), pl.BlockSpec((tm, n - ng), lambda i: (i, 0))],
        out_shape=[jax.ShapeDtypeStruct((t, ng), BF16), jax.ShapeDtypeStruct((t, n - ng), F32)],
        compiler_params=_params("parallel"),
        name="gate_dt_proj",
    )(xn, w)


def _qk_kernel(x_ref, w_ref, cos_ref, sin_ref, o_ref):
    acc = jnp.dot(x_ref[...], w_ref[...], preferred_element_type=F32)
    cos = cos_ref[...]
    sin = sin_ref[...]
    cos_k = cos * (RET_DK ** -0.5)
    sin_k = sin * (RET_DK ** -0.5)
    for h in range(2 * RET_HEADS):
        c, s = (cos, sin) if h < RET_HEADS else (cos_k, sin_k)
        lo = h * RET_DK
        t1 = acc[:, lo:lo + ROPE_HALF]
        t2 = acc[:, lo + ROPE_HALF:lo + RET_DK]
        o_ref[:, lo:lo + ROPE_HALF] = (t1 * c - t2 * s).astype(o_ref.dtype)
        o_ref[:, lo + ROPE_HALF:lo + RET_DK] = (t1 * s + t2 * c).astype(o_ref.dtype)


def _qk_proj(xn, w, layer, cos, sin, seq, tm):
    t, d = xn.shape
    npos = seq // tm
    n = 2 * RET_QK
    return pl.pallas_call(
        _qk_kernel,
        grid=(t // tm,),
        in_specs=[pl.BlockSpec((tm, d), lambda i: (i, 0)),
                  pl.BlockSpec((None, d, n), lambda i: (layer, 0, 0)),
                  pl.BlockSpec((tm, ROPE_HALF), lambda i: (i % npos, 0)),
                  pl.BlockSpec((tm, ROPE_HALF), lambda i: (i % npos, 0))],
        out_specs=pl.BlockSpec((tm, n), lambda i: (i, 0)),
        out_shape=jax.ShapeDtypeStruct((t, n), BF16),
        compiler_params=_params("parallel"),
        name="qk_proj_rotary",
    )(xn, w, cos, sin)


def _shift_rows(a, prev, back):
    n, width = a.shape
    rot = pltpu.roll(a.reshape(n // SUBLANES, SUBLANES, width), back, 1)
    below = jnp.concatenate([pltpu.roll(prev, back, 0)[None], rot[:-1]], axis=0)
    sub = lax.broadcasted_iota(jnp.int32, rot.shape, 1)
    return jnp.where(sub < back, below, rot).reshape(n, width)


def _vxbc_kernel(x_ref, wv_ref, wx_ref, cw_ref, cb_ref, ov_ref, ox_ref, carry_ref, *, tm, seq):
    i = pl.program_id(0)
    j = pl.program_id(1)

    @pl.when(i == 0)
    def _():
        carry_ref[j] = jnp.zeros(carry_ref.shape[1:], F32)

    x = x_ref[...]
    batch_start = lax.rem(i, seq // tm) == 0
    nv = ov_ref.shape[1] // MXU_COLS
    for s in range(ox_ref.shape[1] // MXU_COLS):
        cols = slice(s * MXU_COLS, (s + 1) * MXU_COLS)
        acc = jnp.dot(x, wx_ref[:, cols], preferred_element_type=F32)
        if s < nv:
            ov_ref[:, cols] = jnp.dot(x, wv_ref[:, cols], preferred_element_type=F32).astype(ov_ref.dtype)
        prev = jnp.where(batch_start, 0.0, carry_ref[j, :, cols])
        carry_ref[j, :, cols] = acc[tm - SUBLANES:, :]
        w0, w1, w2, w3 = (cw_ref[k:k + 1, cols] for k in range(SSD_CONV))
        back1 = _shift_rows(acc, prev, 1)
        pair_now = cb_ref[:, cols] + acc * w3 + back1 * w2
        pair_old = acc * w1 + back1 * w0
        pair_old_prev = prev * w1 + pltpu.roll(prev, 1, 0) * w0
        conv = pair_now + _shift_rows(pair_old, pair_old_prev, 2)
        ox_ref[:, cols] = _silu(conv).astype(ox_ref.dtype)


def _vxbc_proj(xn, w, w_xbc, layer, conv_w, conv_b, seq, tm, nsplit):
    t, d = xn.shape
    tnv = RET_V // nsplit
    tnx = SSD_CONV_CH // nsplit
    vblk0 = OFF_V // tnv
    return pl.pallas_call(
        functools.partial(_vxbc_kernel, tm=tm, seq=seq),
        grid=(t // tm, nsplit),
        in_specs=[pl.BlockSpec((tm, d), lambda i, j: (i, 0)),
                  pl.BlockSpec((None, d, tnv), lambda i, j: (layer, 0, vblk0 + j)),
                  pl.BlockSpec((None, d, tnx), lambda i, j: (layer, 0, j)),
                  pl.BlockSpec((None, SSD_CONV, tnx), lambda i, j: (layer, 0, j)),
                  pl.BlockSpec((None, 1, tnx), lambda i, j: (layer, 0, j))],
        out_specs=[pl.BlockSpec((tm, tnv), lambda i, j: (i, j)), pl.BlockSpec((tm, tnx), lambda i, j: (i, j))],
        out_shape=[jax.ShapeDtypeStruct((t, RET_V), BF16), jax.ShapeDtypeStruct((t, SSD_CONV_CH), BF16)],
        scratch_shapes=[pltpu.VMEM((nsplit, SUBLANES, tnx), F32)],
        compiler_params=_params("arbitrary", "arbitrary"),
        name="v_xbc_proj_conv",
    )(xn, w, w_xbc, conv_w, conv_b.reshape(conv_b.shape[0], 1, -1))


def _retention_kernel(q_ref, k_ref, v_ref, g_ref, dmask_ref, qdec_ref, kdec_ref, cdec_ref, o_ref, state_ref):
    @pl.when(pl.program_id(1) == 0)
    def _():
        state_ref[...] = jnp.zeros_like(state_ref)

    contract_last = (((1,), (1,)), ((), ()))
    contract_first = (((0,), (0,)), ((), ()))
    for c in range(RET_BLOCK // RET_CHUNK):
        rows = pl.ds(c * RET_CHUNK, RET_CHUNK)
        for h in range(RET_HEADS):
            q = q_ref[rows, h * RET_DK:(h + 1) * RET_DK]
            k = k_ref[rows, h * RET_DK:(h + 1) * RET_DK]
            v = v_ref[rows, h * RET_DV:(h + 1) * RET_DV]
            s = lax.dot_general(q, k, contract_last, preferred_element_type=F32) * dmask_ref[h]
            st = state_ref[h]
            lhs = jnp.concatenate([s.astype(BF16), q * qdec_ref[h].astype(BF16)], axis=1)
            rhs = jnp.concatenate([v, st.astype(BF16)], axis=0)
            y = jnp.dot(lhs, rhs, preferred_element_type=F32)
            kd = k * kdec_ref[h].astype(BF16)
            state_ref[h] = st * cdec_ref[h] + lax.dot_general(kd, v, contract_first, preferred_element_type=F32)
            yn = (y * lax.rsqrt(jnp.mean(y * y, axis=-1, keepdims=True) + EPS)).astype(BF16)
            o_ref[rows, h * RET_DV:(h + 1) * RET_DV] = g_ref[rows, h * RET_DV:(h + 1) * RET_DV] * yn


def _retention(qk, v, gz, dmask, qdec, kdec, cdec, batch, seq):
    t = batch * seq
    nblk = seq // RET_BLOCK
    row = lambda b, i: b * nblk + i
    return pl.pallas_call(
        _retention_kernel,
        grid=(batch, nblk),
        in_specs=[pl.BlockSpec((RET_BLOCK, RET_QK), lambda b, i: (row(b, i), 0)),
                  pl.BlockSpec((RET_BLOCK, RET_QK), lambda b, i: (row(b, i), 1)),
                  pl.BlockSpec((RET_BLOCK, RET_V), lambda b, i: (row(b, i), 0)),
                  pl.BlockSpec((RET_BLOCK, RET_V), lambda b, i: (row(b, i), 0)),
                  _resident((RET_HEADS, RET_CHUNK, RET_CHUNK)),
                  _resident((RET_HEADS, RET_CHUNK, 1)),
                  _resident((RET_HEADS, RET_CHUNK, 1)),
                  _resident((RET_HEADS, 1, 1))],
        out_specs=pl.BlockSpec((RET_BLOCK, RET_V), lambda b, i: (row(b, i), 0)),
        out_shape=jax.ShapeDtypeStruct((t, RET_V), BF16),
        scratch_shapes=[pltpu.VMEM((RET_HEADS, RET_DK, RET_DV), F32)],
        compiler_params=_params("parallel", "arbitrary"),
        name="retention_scan",
    )(qk, qk, v, gz, dmask, qdec, kdec, cdec)


def _retention_tables():
    log_g = jnp.log(1.0 - 2.0 ** (-5.0 - jnp.arange(RET_HEADS, dtype=F32)))
    idx = jnp.arange(RET_CHUNK, dtype=F32)
    diff = idx[:, None] - idx[None, :]
    dmask = jnp.where((diff >= 0)[None], jnp.exp(log_g[:, None, None] * jnp.maximum(diff, 0.0)[None]), 0.0)
    qdec = jnp.exp(log_g[:, None] * (idx + 1.0))[:, :, None]
    kdec = jnp.exp(log_g[:, None] * (RET_CHUNK - 1.0 - idx))[:, :, None]
    cdec = jnp.exp(log_g * RET_CHUNK)[:, None, None]
    return dmask, qdec, kdec, cdec


def _cumsum_rows(x):
    n = x.shape[0]
    row = lax.broadcasted_iota(jnp.int32, x.shape, 0)
    shift = 1
    while shift < n:
        x = x + jnp.where(row >= shift, pltpu.roll(x, shift, 0), 0.0)
        shift *= 2
    return x


def _ssd_kernel(xc_ref, dt_ref, dtb_ref, alog_ref, o_ref, state_ref):
    @pl.when(pl.program_id(1) == 0)
    def _():
        state_ref[...] = jnp.zeros_like(state_ref)

    for cc in range(SSD_BLOCK // SSD_CHUNK):
        rows = pl.ds(cc * SSD_CHUNK, SSD_CHUNK)
        _ssd_chunk(xc_ref.at[rows], dt_ref.at[rows], dtb_ref, alog_ref, o_ref.at[rows], state_ref)


def _ssd_chunk(xc_ref, dt_ref, dtb_ref, alog_ref, o_ref, state_ref):
    c = SSD_CHUNK
    dtp = _softplus(dt_ref[...] + dtb_ref[...])
    la = dtp * (-jnp.exp(alog_ref[...]))
    acs = _cumsum_rows(la) * LOG2_E
    acs_t = acs.T
    adj_t = acs_t - jnp.log2(dtp.T)

    row = lax.broadcasted_iota(jnp.int32, (c, c), 0)
    col = lax.broadcasted_iota(jnp.int32, (c, c), 1)
    causal = row >= col
    left = lax.broadcasted_iota(jnp.int32, (1, LANES), 1) < SSD_HEADDIM
    contract_last = (((1,), (1,)), ((), ()))

    for g in range(SSD_GROUPS):
        b_bf = xc_ref[:, SSD_INNER + g * SSD_STATE:SSD_INNER + (g + 1) * SSD_STATE]
        c_bf = xc_ref[:, SSD_INNER + SSD_BC + g * SSD_STATE:SSD_INNER + SSD_BC + (g + 1) * SSD_STATE]
        cb_g = lax.dot_general(c_bf, b_bf, contract_last, preferred_element_type=F32).astype(BF16)
        b_t = b_bf.astype(F32).T.astype(BF16)
        for pp in range(PAIRS_PER_GROUP):
            p = g * PAIRS_PER_GROUP + pp
            lhs_y, lhs_s, decs = [], [], []
            for r in (HEADS_PER_VREG * p, HEADS_PER_VREG * p + 1):
                a_i = jnp.broadcast_to(acs[:, r:r + 1], (c, c))
                adj_j = adj_t[r:r + 1, :]
                lhs_y.append(cb_g * jnp.exp2(jnp.where(causal, a_i - adj_j, -1e30)).astype(BF16))
                lhs_y.append(c_bf * jnp.exp2(a_i).astype(BF16))
                a_last = acs_t[r:r + 1, c - 1:c]
                lhs_s.append(b_t * jnp.exp2(a_last - adj_j).astype(BF16))
                decs.append(jnp.exp2(a_last))
            xs_p = xc_ref[:, p * LANES:(p + 1) * LANES]
            zero_x = jnp.zeros_like(xs_p)
            xs_l = jnp.where(left, xs_p, zero_x)
            xs_r = jnp.where(left, zero_x, xs_p)
            st = state_ref[p]
            st_bf = st.astype(BF16)
            zero_s = jnp.zeros_like(st_bf)
            st_l = jnp.where(left, st_bf, zero_s)
            st_r = jnp.where(left, zero_s, st_bf)
            y = jnp.dot(jnp.concatenate(lhs_y, axis=1), jnp.concatenate([xs_l, st_l, xs_r, st_r], axis=0),
                        preferred_element_type=F32)
            upd = jnp.dot(jnp.concatenate(lhs_s, axis=1), jnp.concatenate([xs_l, xs_r], axis=0),
                          preferred_element_type=F32)
            dec = jnp.where(left, decs[0], decs[1])
            state_ref[p] = st * dec + upd
            o_ref[:, p * LANES:(p + 1) * LANES] = y.astype(o_ref.dtype)


def _ssd(xc, dt, dt_bias, a_log, batch, seq):
    t = batch * seq
    c = SSD_BLOCK
    nblk = seq // c
    row = lambda b, i: b * nblk + i
    pad = LANES - SSD_HEADS
    return pl.pallas_call(
        _ssd_kernel,
        grid=(batch, nblk),
        in_specs=[pl.BlockSpec((c, SSD_CONV_CH), lambda b, i: (row(b, i), 0)),
                  pl.BlockSpec((c, LANES), lambda b, i: (row(b, i), 0)),
                  _resident((1, LANES)),
                  _resident((1, LANES))],
        out_specs=pl.BlockSpec((c, SSD_INNER), lambda b, i: (row(b, i), 0)),
        out_shape=jax.ShapeDtypeStruct((t, SSD_INNER), BF16),
        scratch_shapes=[pltpu.VMEM((SSD_PAIRS, SSD_STATE, LANES), F32)],
        compiler_params=_params("parallel", "arbitrary"),
        name="ssd_scan",
    )(xc, dt, jnp.pad(dt_bias, (0, pad)).reshape(1, LANES), jnp.pad(a_log, (0, pad)).reshape(1, LANES))


def _merge_kernel(yr_ref, y_ref, xs_ref, sz_ref, gr_ref, gs_ref, h_ref, dskip_ref, nw_ref, wr_ref, ws_ref, wo_ref,
                  o_ref):
    o_ret = jnp.dot(yr_ref[...], wr_ref[...], preferred_element_type=F32)
    gw = SSD_INNER // SSD_GROUPS
    o_ssd = None
    for g in range(SSD_GROUPS):
        cols = slice(g * gw, (g + 1) * gw)
        y = y_ref[:, cols].astype(F32) + dskip_ref[:, cols] * xs_ref[:, cols].astype(F32)
        y = y * sz_ref[:, cols].astype(F32)
        part = jnp.dot(_rms(y, nw_ref[:, cols]).astype(BF16), ws_ref[cols, :], preferred_element_type=F32)
        o_ssd = part if o_ssd is None else o_ssd + part
    merged = gr_ref[...].astype(F32) * o_ret + gs_ref[...].astype(F32) * o_ssd
    o_ref[...] = h_ref[...] + jnp.dot(merged.astype(BF16), wo_ref[...], preferred_element_type=F32)


def _merge(yr, y_ssd, xc, gz, gates, h, d_skip, norm_w, w_ret, w_ssd, w_o, layer, tm):
    t, d = h.shape
    rows = lambda i: (i, 0)
    return pl.pallas_call(
        _merge_kernel,
        grid=(t // tm,),
        in_specs=[pl.BlockSpec((tm, RET_V), rows),
                  pl.BlockSpec((tm, SSD_INNER), rows),
                  pl.BlockSpec((tm, SSD_INNER), rows),
                  pl.BlockSpec((tm, SSD_INNER), lambda i: (i, 1)),
                  pl.BlockSpec((tm, d), lambda i: (i, 0)),
                  pl.BlockSpec((tm, d), lambda i: (i, 1)),
                  pl.BlockSpec((tm, d), rows),
                  _resident((1, SSD_INNER)),
                  _resident((1, SSD_INNER)),
                  _resident((RET_V, d), layer),
                  _resident((SSD_INNER, d), layer),
                  _resident((d, d), layer)],
        out_specs=pl.BlockSpec((tm, d), rows),
        out_shape=jax.ShapeDtypeStruct((t, d), F32),
        compiler_params=_params("parallel"),
        name="out_proj_merge",
    )(yr, y_ssd, xc, gz, gates, gates, h,
      jnp.repeat(d_skip, SSD_HEADDIM).reshape(1, SSD_INNER), norm_w.reshape(1, SSD_INNER), w_ret, w_ssd, w_o)


def _ffn_kernel(h_ref, nw_ref, wgu_ref, wd_ref, nw2_ref, o_ref, on_ref):
    x = h_ref[...]
    xn = _rms(x, nw_ref[...]).astype(BF16)
    gate = jnp.dot(xn, wgu_ref[:, :D_FF], preferred_element_type=F32)
    up = jnp.dot(xn, wgu_ref[:, D_FF:], preferred_element_type=F32)
    out = x + jnp.dot((_silu(gate) * up).astype(BF16), wd_ref[...], preferred_element_type=F32)
    o_ref[...] = out
    on_ref[...] = _rms(out, nw2_ref[...]).astype(on_ref.dtype)


def _ffn(h, norm_w, w_gate_up, w_down, layer, next_norm_w, next_dtype, tm):
    t, d = h.shape
    rows = lambda i: (i, 0)
    return pl.pallas_call(
        _ffn_kernel,
        grid=(t // tm,),
        in_specs=[pl.BlockSpec((tm, d), rows),
                  _resident((1, d)),
                  _resident((d, 2 * D_FF), layer),
                  _resident((D_FF, d), layer),
                  _resident((1, d))],
        out_specs=[pl.BlockSpec((tm, d), rows), pl.BlockSpec((tm, d), rows)],
        out_shape=[jax.ShapeDtypeStruct((t, d), F32), jax.ShapeDtypeStruct((t, d), next_dtype)],
        compiler_params=_params("parallel"),
        name="swiglu_ffn",
    )(h, norm_w.reshape(1, d), w_gate_up, w_down, next_norm_w.reshape(1, d))


def kernel(x, norm_mix_w, w_in, ret_out, conv_w, conv_b, dt_bias, a_log, d_skip, ssd_norm_w, ssd_out, w_o,
           norm_ffn_w, w_gate_up, w_down, final_norm_w):
    batch, seq, d = x.shape
    depth = w_in.shape[0]
    assert d == D_MODEL and w_in.shape[2] == D_IN
    assert seq % RET_BLOCK == 0 and seq % SSD_BLOCK == 0
    t = batch * seq
    tm = min(PROJ_TM, seq)

    inv = (ROPE_BASE ** (-jnp.arange(ROPE_HALF, dtype=F32) / ROPE_HALF)).reshape(1, ROPE_HALF)
    cos, sin = _rope_table(inv, seq, tm)
    dmask, qdec, kdec, cdec = _retention_tables()

    w = w_in.astype(BF16)
    w_xbc = w[:, :, OFF_XBC:OFF_DT]
    w_gates_dt = jnp.concatenate(
        [w[:, :, OFF_GATES:], jnp.pad(w[:, :, OFF_DT:OFF_GATES], ((0, 0), (0, 0), (0, LANES - SSD_HEADS)))], axis=2)
    w_ret, w_ssd, w_out = ret_out.astype(BF16), ssd_out.astype(BF16), w_o.astype(BF16)
    w_gu, w_dn = w_gate_up.astype(BF16), w_down.astype(BF16)

    h = x.reshape(t, d)
    xn = _rmsnorm(h, norm_mix_w[0], BF16, tm)
    out = None
    for l in range(depth):
        qk = _qk_proj(xn, w, l, cos, sin, seq, tm)
        v, xc = _vxbc_proj(xn, w, w_xbc, l, conv_w, conv_b, seq, tm, VXBC_SPLIT)
        gz = _proj(xn, w, l, OFF_G, RET_V + SSD_INNER, "silu", BF16, tm, GZ_TN, "gz_proj")
        gates, dt = _gates_dt_proj(xn, w_gates_dt, l, tm)

        yr = _retention(qk, v, gz, dmask, qdec, kdec, cdec, batch, seq)
        y_ssd = _ssd(xc, dt, dt_bias[l], a_log[l], batch, seq)
        h = _merge(yr, y_ssd, xc, gz, gates, h, d_skip[l], ssd_norm_w[l], w_ret, w_ssd, w_out, l,
                   min(MERGE_TM, seq))

        last = l == depth - 1
        next_w = final_norm_w if last else norm_mix_w[l + 1]
        h, nxt = _ffn(h, norm_ffn_w[l], w_gu, w_dn, l, next_w, F32 if last else BF16, min(FFN_TM, seq))
        if last:
            out = nxt
        else:
            xn = nxt
    return out.reshape(batch, seq, d)
```

```python
import functools

import jax
import jax.numpy as jnp
from jax import lax
from jax.experimental import pallas as pl
from jax.experimental.pallas import tpu as pltpu

EPS = 1e-6
D_MODEL = 1024

RET_HEADS = 4
RET_DK = 256
RET_DV = 512
RET_QK = RET_HEADS * RET_DK
RET_V = RET_HEADS * RET_DV
ROPE_BASE = 10000.0
ROPE_HALF = RET_DK // 2

SSD_INNER = 2 * D_MODEL
SSD_HEADDIM = 64
SSD_HEADS = SSD_INNER // SSD_HEADDIM
SSD_GROUPS = 4
SSD_HPG = SSD_HEADS // SSD_GROUPS
SSD_STATE = 128
SSD_CONV = 4
SSD_BC = SSD_GROUPS * SSD_STATE
SSD_CONV_CH = SSD_INNER + 2 * SSD_BC

D_FF = 2816

OFF_V = 2 * RET_QK
OFF_G = OFF_V + RET_V
OFF_XBC = OFF_G + RET_V + SSD_INNER
OFF_DT = OFF_XBC + SSD_CONV_CH
OFF_GATES = OFF_DT + SSD_HEADS
D_IN = OFF_GATES + 2 * D_MODEL

LANES = 128
SUBLANES = 8
MXU_COLS = 256
SSD_CHUNK = 128
SSD_BLOCK = 512
RET_CHUNK = 256
RET_BLOCK = 512
HEADS_PER_VREG = LANES // SSD_HEADDIM
SSD_PAIRS = SSD_HEADS // HEADS_PER_VREG
PAIRS_PER_GROUP = SSD_HPG // HEADS_PER_VREG

PROJ_TM = 1024
GZ_TN = 4096
VXBC_SPLIT = 2
MERGE_TM = 512
FFN_TM = 512

VMEM_LIMIT_BYTES = 56 * 1024 * 1024

LOG2_E = 1.4426950408889634

BF16 = jnp.bfloat16
F32 = jnp.float32


def _params(*semantics):
    return pltpu.CompilerParams(dimension_semantics=semantics, vmem_limit_bytes=VMEM_LIMIT_BYTES)


def _resident(shape, layer=None):
    if layer is None:
        return pl.BlockSpec(shape, lambda *_: (0,) * len(shape), pipeline_mode=pl.Buffered(1))
    return pl.BlockSpec((None,) + shape, lambda *_: (layer,) + (0,) * len(shape), pipeline_mode=pl.Buffered(1))


def _sigmoid(x):
    return 0.5 + 0.5 * jnp.tanh(0.5 * x)


def _silu(x):
    h = 0.5 * x
    return h + h * jnp.tanh(h)


def _softplus(x):
    return jnp.maximum(x, 0.0) + jnp.log(1.0 + jnp.exp(-jnp.abs(x)))


def _rms(x, w):
    return x * lax.rsqrt(jnp.mean(x * x, axis=-1, keepdims=True) + EPS) * w


_ACTIVATIONS = {"none": lambda a: a, "silu": _silu, "sigmoid": _sigmoid}


def _rmsnorm_kernel(x_ref, w_ref, o_ref):
    o_ref[...] = _rms(x_ref[...], w_ref[...]).astype(o_ref.dtype)


def _rmsnorm(x, w, out_dtype, tm):
    t, d = x.shape
    return pl.pallas_call(
        _rmsnorm_kernel,
        grid=(t // tm,),
        in_specs=[pl.BlockSpec((tm, d), lambda i: (i, 0)), pl.BlockSpec((1, d), lambda i: (0, 0))],
        out_specs=pl.BlockSpec((tm, d), lambda i: (i, 0)),
        out_shape=jax.ShapeDtypeStruct((t, d), out_dtype),
        compiler_params=_params("parallel"),
        name="rmsnorm",
    )(x, w.reshape(1, d))


def _rope_kernel(inv_ref, cos_ref, sin_ref, *, tm):
    pos = (pl.program_id(0) * tm + lax.broadcasted_iota(jnp.int32, (tm, ROPE_HALF), 0)).astype(F32)
    ang = pos * inv_ref[...]
    cos_ref[...] = jnp.cos(ang)
    sin_ref[...] = jnp.sin(ang)


def _rope_table(inv, seq, tm):
    spec = pl.BlockSpec((tm, ROPE_HALF), lambda i: (i, 0))
    shape = jax.ShapeDtypeStruct((seq, ROPE_HALF), F32)
    return pl.pallas_call(
        functools.partial(_rope_kernel, tm=tm),
        grid=(seq // tm,),
        in_specs=[pl.BlockSpec((1, ROPE_HALF), lambda i: (0, 0))],
        out_specs=[spec, spec],
        out_shape=[shape, shape],
        compiler_params=_params("parallel"),
        name="rope_table",
    )(inv)


def _proj_kernel(x_ref, w_ref, o_ref, *, act):
    acc = jnp.dot(x_ref[...], w_ref[...], preferred_element_type=F32)
    o_ref[...] = _ACTIVATIONS[act](acc).astype(o_ref.dtype)


def _proj(xn, w, layer, col0, ncols, act, out_dtype, tm, tn, name):
    t, d = xn.shape
    blk0 = col0 // tn
    return pl.pallas_call(
        functools.partial(_proj_kernel, act=act),
        grid=(t // tm, ncols // tn),
        in_specs=[pl.BlockSpec((tm, d), lambda i, j: (i, 0)),
                  pl.BlockSpec((None, d, tn), lambda i, j: (layer, 0, blk0 + j))],
        out_specs=pl.BlockSpec((tm, tn), lambda i, j: (i, j)),
        out_shape=jax.ShapeDtypeStruct((t, ncols), out_dtype),
        compiler_params=_params("parallel", "arbitrary"),
        name=name,
    )(xn, w)


def _gates_dt_kernel(x_ref, w_ref, og_ref, odt_ref):
    acc = jnp.dot(x_ref[...], w_ref[...], preferred_element_type=F32)
    ng = og_ref.shape[1]
    og_ref[...] = _sigmoid(acc[:, :ng]).astype(og_ref.dtype)
    odt_ref[...] = acc[:, ng:]


def _gates_dt_proj(xn, w, layer, tm):
    t, d = xn.shape
    ng = 2 * D_MODEL
    n = w.shape[2]
    return pl.pallas_call(
        _gates_dt_kernel,
        grid=(t // tm,),
        in_specs=[pl.BlockSpec((tm, d), lambda i: (i, 0)), pl.BlockSpec((None, d, n), lambda i: (layer, 0, 0))],
        out_specs=[pl.BlockSpec((tm, ng), lambda i: (i, 0)), pl.BlockSpec((tm, n - ng), lambda i: (i, 0))],
        out_shape=[jax.ShapeDtypeStruct((t, ng), BF16), jax.ShapeDtypeStruct((t, n - ng), F32)],
        compiler_params=_params("parallel"),
        name="gate_dt_proj",
    )(xn, w)


def _qk_kernel(x_ref, w_ref, cos_ref, sin_ref, o_ref):
    acc = jnp.dot(x_ref[...], w_ref[...], preferred_element_type=F32)
    cos = cos_ref[...]
    sin = sin_ref[...]
    cos_k = cos * (RET_DK ** -0.5)
    sin_k = sin * (RET_DK ** -0.5)
    for h in range(2 * RET_HEADS):
        c, s = (cos, sin) if h < RET_HEADS else (cos_k, sin_k)
        lo = h * RET_DK
        t1 = acc[:, lo:lo + ROPE_HALF]
        t2 = acc[:, lo + ROPE_HALF:lo + RET_DK]
        o_ref[:, lo:lo + ROPE_HALF] = (t1 * c - t2 * s).astype(o_ref.dtype)
        o_ref[:, lo + ROPE_HALF:lo + RET_DK] = (t1 * s + t2 * c).astype(o_ref.dtype)


def _qk_proj(xn, w, layer, cos, sin, seq, tm):
    t, d = xn.shape
    npos = seq // tm
    n = 2 * RET_QK
    return pl.pallas_call(
        _qk_kernel,
        grid=(t // tm,),
        in_specs=[pl.BlockSpec((tm, d), lambda i: (i, 0)),
                  pl.BlockSpec((None, d, n), lambda i: (layer, 0, 0)),
                  pl.BlockSpec((tm, ROPE_HALF), lambda i: (i % npos, 0)),
                  pl.BlockSpec((tm, ROPE_HALF), lambda i: (i % npos, 0))],
        out_specs=pl.BlockSpec((tm, n), lambda i: (i, 0)),
        out_shape=jax.ShapeDtypeStruct((t, n), BF16),
        compiler_params=_params("parallel"),
        name="qk_proj_rotary",
    )(xn, w, cos, sin)


def _shift_rows(a, prev, back):
    n, width = a.shape
    rot = pltpu.roll(a.reshape(n // SUBLANES, SUBLANES, width), back, 1)
    below = jnp.concatenate([pltpu.roll(prev, back, 0)[None], rot[:-1]], axis=0)
    sub = lax.broadcasted_iota(jnp.int32, rot.shape, 1)
    return jnp.where(sub < back, below, rot).reshape(n, width)


def _vxbc_kernel(x_ref, wv_ref, wx_ref, cw_ref, cb_ref, ov_ref, ox_ref, carry_ref, *, tm, seq):
    i = pl.program_id(0)
    j = pl.program_id(1)

    @pl.when(i == 0)
    def _():
        carry_ref[j] = jnp.zeros(carry_ref.shape[1:], F32)

    x = x_ref[...]
    batch_start = lax.rem(i, seq // tm) == 0
    nv = ov_ref.shape[1] // MXU_COLS
    for s in range(ox_ref.shape[1] // MXU_COLS):
        cols = slice(s * MXU_COLS, (s + 1) * MXU_COLS)
        acc = jnp.dot(x, wx_ref[:, cols], preferred_element_type=F32)
        if s < nv:
            ov_ref[:, cols] = jnp.dot(x, wv_ref[:, cols], preferred_element_type=F32).astype(ov_ref.dtype)
        prev = jnp.where(batch_start, 0.0, carry_ref[j, :, cols])
        carry_ref[j, :, cols] = acc[tm - SUBLANES:, :]
        w0, w1, w2, w3 = (cw_ref[k:k + 1, cols] for k in range(SSD_CONV))
        back1 = _shift_rows(acc, prev, 1)
        pair_now = cb_ref[:, cols] + acc * w3 + back1 * w2
        pair_old = acc * w1 + back1 * w0
        pair_old_prev = prev * w1 + pltpu.roll(prev, 1, 0) * w0
        conv = pair_now + _shift_rows(pair_old, pair_old_prev, 2)
        ox_ref[:, cols] = _silu(conv).astype(ox_ref.dtype)


def _vxbc_proj(xn, w, w_xbc, layer, conv_w, conv_b, seq, tm, nsplit):
    t, d = xn.shape
    tnv = RET_V // nsplit
    tnx = SSD_CONV_CH // nsplit
    vblk0 = OFF_V // tnv
    return pl.pallas_call(
        functools.partial(_vxbc_kernel, tm=tm, seq=seq),
        grid=(t // tm, nsplit),
        in_specs=[pl.BlockSpec((tm, d), lambda i, j: (i, 0)),
                  pl.BlockSpec((None, d, tnv), lambda i, j: (layer, 0, vblk0 + j)),
                  pl.BlockSpec((None, d, tnx), lambda i, j: (layer, 0, j)),
                  pl.BlockSpec((None, SSD_CONV, tnx), lambda i, j: (layer, 0, j)),
                  pl.BlockSpec((None, 1, tnx), lambda i, j: (layer, 0, j))],
        out_specs=[pl.BlockSpec((tm, tnv), lambda i, j: (i, j)), pl.BlockSpec((tm, tnx), lambda i, j: (i, j))],
        out_shape=[jax.ShapeDtypeStruct((t, RET_V), BF16), jax.ShapeDtypeStruct((t, SSD_CONV_CH), BF16)],
        scratch_shapes=[pltpu.VMEM((nsplit, SUBLANES, tnx), F32)],
        compiler_params=_params("arbitrary", "arbitrary"),
        name="v_xbc_proj_conv",
    )(xn, w, w_xbc, conv_w, conv_b.reshape(conv_b.shape[0], 1, -1))


def _retention_block(q_ref, k_ref, v_ref, g_ref, dmask_ref, qdec_ref, kdec_ref, cdec_ref, o_ref, state_ref):
    contract_last = (((1,), (1,)), ((), ()))
    contract_first = (((0,), (0,)), ((), ()))
    for c in range(RET_BLOCK // RET_CHUNK):
        rows = pl.ds(c * RET_CHUNK, RET_CHUNK)
        for h in range(RET_HEADS):
            q = q_ref[rows, h * RET_DK:(h + 1) * RET_DK]
            k = k_ref[rows, h * RET_DK:(h + 1) * RET_DK]
            v = v_ref[rows, h * RET_DV:(h + 1) * RET_DV]
            s = lax.dot_general(q, k, contract_last, preferred_element_type=F32) * dmask_ref[h]
            st = state_ref[h]
            lhs = jnp.concatenate([s.astype(BF16), q * qdec_ref[h].astype(BF16)], axis=1)
            rhs = jnp.concatenate([v, st.astype(BF16)], axis=0)
            y = jnp.dot(lhs, rhs, preferred_element_type=F32)
            kd = k * kdec_ref[h].astype(BF16)
            state_ref[h] = st * cdec_ref[h] + lax.dot_general(kd, v, contract_first, preferred_element_type=F32)
            yn = (y * lax.rsqrt(jnp.mean(y * y, axis=-1, keepdims=True) + EPS)).astype(BF16)
            o_ref[rows, h * RET_DV:(h + 1) * RET_DV] = g_ref[rows, h * RET_DV:(h + 1) * RET_DV] * yn


def _retention_tables():
    log_g = jnp.log(1.0 - 2.0 ** (-5.0 - jnp.arange(RET_HEADS, dtype=F32)))
    idx = jnp.arange(RET_CHUNK, dtype=F32)
    diff = idx[:, None] - idx[None, :]
    dmask = jnp.where((diff >= 0)[None], jnp.exp(log_g[:, None, None] * jnp.maximum(diff, 0.0)[None]), 0.0)
    qdec = jnp.exp(log_g[:, None] * (idx + 1.0))[:, :, None]
    kdec = jnp.exp(log_g[:, None] * (RET_CHUNK - 1.0 - idx))[:, :, None]
    cdec = jnp.exp(log_g * RET_CHUNK)[:, None, None]
    return dmask, qdec, kdec, cdec


def _cumsum_rows(x):
    n = x.shape[0]
    row = lax.broadcasted_iota(jnp.int32, x.shape, 0)
    shift = 1
    while shift < n:
        x = x + jnp.where(row >= shift, pltpu.roll(x, shift, 0), 0.0)
        shift *= 2
    return x


def _ssd_block(xc_ref, dt_ref, dtb_ref, alog_ref, o_ref, state_ref):
    for cc in range(SSD_BLOCK // SSD_CHUNK):
        rows = pl.ds(cc * SSD_CHUNK, SSD_CHUNK)
        _ssd_chunk(xc_ref.at[rows], dt_ref.at[rows], dtb_ref, alog_ref, o_ref.at[rows], state_ref)


def _ssd_chunk(xc_ref, dt_ref, dtb_ref, alog_ref, o_ref, state_ref):
    c = SSD_CHUNK
    dtp = _softplus(dt_ref[...] + dtb_ref[...])
    la = dtp * (-jnp.exp(alog_ref[...]))
    acs = _cumsum_rows(la) * LOG2_E
    acs_t = acs.T
    adj_t = acs_t - jnp.log2(dtp.T)

    row = lax.broadcasted_iota(jnp.int32, (c, c), 0)
    col = lax.broadcasted_iota(jnp.int32, (c, c), 1)
    causal = row >= col
    left = lax.broadcasted_iota(jnp.int32, (1, LANES), 1) < SSD_HEADDIM
    contract_last = (((1,), (1,)), ((), ()))

    for g in range(SSD_GROUPS):
        b_bf = xc_ref[:, SSD_INNER + g * SSD_STATE:SSD_INNER + (g + 1) * SSD_STATE]
        c_bf = xc_ref[:, SSD_INNER + SSD_BC + g * SSD_STATE:SSD_INNER + SSD_BC + (g + 1) * SSD_STATE]
        cb_g = lax.dot_general(c_bf, b_bf, contract_last, preferred_element_type=F32).astype(BF16)
        b_t = b_bf.astype(F32).T.astype(BF16)
        for pp in range(PAIRS_PER_GROUP):
            p = g * PAIRS_PER_GROUP + pp
            lhs_y, lhs_s, decs = [], [], []
            for r in (HEADS_PER_VREG * p, HEADS_PER_VREG * p + 1):
                a_i = jnp.broadcast_to(acs[:, r:r + 1], (c, c))
                adj_j = adj_t[r:r + 1, :]
                lhs_y.append(cb_g * jnp.exp2(jnp.where(causal, a_i - adj_j, -1e30)).astype(BF16))
                lhs_y.append(c_bf * jnp.exp2(a_i).astype(BF16))
                a_last = acs_t[r:r + 1, c - 1:c]
                lhs_s.append(b_t * jnp.exp2(a_last - adj_j).astype(BF16))
                decs.append(jnp.exp2(a_last))
            xs_p = xc_ref[:, p * LANES:(p + 1) * LANES]
            zero_x = jnp.zeros_like(xs_p)
            xs_l = jnp.where(left, xs_p, zero_x)
            xs_r = jnp.where(left, zero_x, xs_p)
            st = state_ref[p]
            st_bf = st.astype(BF16)
            zero_s = jnp.zeros_like(st_bf)
            st_l = jnp.where(left, st_bf, zero_s)
            st_r = jnp.where(left, zero_s, st_bf)
            y = jnp.dot(jnp.concatenate(lhs_y, axis=1), jnp.concatenate([xs_l, st_l, xs_r, st_r], axis=0),
                        preferred_element_type=F32)
            upd = jnp.dot(jnp.concatenate(lhs_s, axis=1), jnp.concatenate([xs_l, xs_r], axis=0),
                          preferred_element_type=F32)
            dec = jnp.where(left, decs[0], decs[1])
            state_ref[p] = st * dec + upd
            o_ref[:, p * LANES:(p + 1) * LANES] = y.astype(o_ref.dtype)


def _scan_kernel(q_ref, k_ref, v_ref, g_ref, dmask_ref, qdec_ref, kdec_ref, cdec_ref, xc_ref, dt_ref, dtb_ref,
                 alog_ref, oret_ref, ossd_ref, rstate_ref, sstate_ref):
    @pl.when(pl.program_id(1) == 0)
    def _():
        rstate_ref[...] = jnp.zeros_like(rstate_ref)
        sstate_ref[...] = jnp.zeros_like(sstate_ref)

    _retention_block(q_ref, k_ref, v_ref, g_ref, dmask_ref, qdec_ref, kdec_ref, cdec_ref, oret_ref, rstate_ref)
    _ssd_block(xc_ref, dt_ref, dtb_ref, alog_ref, ossd_ref, sstate_ref)


def _scans(qk, v, gz, dmask, qdec, kdec, cdec, xc, dt, dt_bias, a_log, batch, seq):
    assert RET_BLOCK == SSD_BLOCK
    t = batch * seq
    blk = RET_BLOCK
    nblk = seq // blk
    row = lambda b, i: b * nblk + i
    pad = LANES - SSD_HEADS
    return pl.pallas_call(
        _scan_kernel,
        grid=(batch, nblk),
        in_specs=[pl.BlockSpec((blk, RET_QK), lambda b, i: (row(b, i), 0)),
                  pl.BlockSpec((blk, RET_QK), lambda b, i: (row(b, i), 1)),
                  pl.BlockSpec((blk, RET_V), lambda b, i: (row(b, i), 0)),
                  pl.BlockSpec((blk, RET_V), lambda b, i: (row(b, i), 0)),
                  _resident((RET_HEADS, RET_CHUNK, RET_CHUNK)),
                  _resident((RET_HEADS, RET_CHUNK, 1)),
                  _resident((RET_HEADS, RET_CHUNK, 1)),
                  _resident((RET_HEADS, 1, 1)),
                  pl.BlockSpec((blk, SSD_CONV_CH), lambda b, i: (row(b, i), 0)),
                  pl.BlockSpec((blk, LANES), lambda b, i: (row(b, i), 0)),
                  _resident((1, LANES)),
                  _resident((1, LANES))],
        out_specs=[pl.BlockSpec((blk, RET_V), lambda b, i: (row(b, i), 0)),
                   pl.BlockSpec((blk, SSD_INNER), lambda b, i: (row(b, i), 0))],
        out_shape=[jax.ShapeDtypeStruct((t, RET_V), BF16), jax.ShapeDtypeStruct((t, SSD_INNER), BF16)],
        scratch_shapes=[pltpu.VMEM((RET_HEADS, RET_DK, RET_DV), F32),
                        pltpu.VMEM((SSD_PAIRS, SSD_STATE, LANES), F32)],
        compiler_params=_params("parallel", "arbitrary"),
        name="retention_ssd_scan",
    )(qk, qk, v, gz, dmask, qdec, kdec, cdec, xc, dt,
      jnp.pad(dt_bias, (0, pad)).reshape(1, LANES), jnp.pad(a_log, (0, pad)).reshape(1, LANES))


def _merge_kernel(yr_ref, y_ref, xs_ref, sz_ref, gr_ref, gs_ref, h_ref, dskip_ref, nw_ref, wr_ref, ws_ref, wo_ref,
                  o_ref):
    o_ret = jnp.dot(yr_ref[...], wr_ref[...], preferred_element_type=F32)
    gw = SSD_INNER // SSD_GROUPS
    o_ssd = None
    for g in range(SSD_GROUPS):
        cols = slice(g * gw, (g + 1) * gw)
        y = y_ref[:, cols].astype(F32) + dskip_ref[:, cols] * xs_ref[:, cols].astype(F32)
        y = y * sz_ref[:, cols].astype(F32)
        part = jnp.dot(_rms(y, nw_ref[:, cols]).astype(BF16), ws_ref[cols, :], preferred_element_type=F32)
        o_ssd = part if o_ssd is None else o_ssd + part
    merged = gr_ref[...].astype(F32) * o_ret + gs_ref[...].astype(F32) * o_ssd
    o_ref[...] = h_ref[...] + jnp.dot(merged.astype(BF16), wo_ref[...], preferred_element_type=F32)


def _merge(yr, y_ssd, xc, gz, gates, h, d_skip, norm_w, w_ret, w_ssd, w_o, layer, tm):
    t, d = h.shape
    rows = lambda i: (i, 0)
    return pl.pallas_call(
        _merge_kernel,
        grid=(t // tm,),
        in_specs=[pl.BlockSpec((tm, RET_V), rows),
                  pl.BlockSpec((tm, SSD_INNER), rows),
                  pl.BlockSpec((tm, SSD_INNER), rows),
                  pl.BlockSpec((tm, SSD_INNER), lambda i: (i, 1)),
                  pl.BlockSpec((tm, d), lambda i: (i, 0)),
                  pl.BlockSpec((tm, d), lambda i: (i, 1)),
                  pl.BlockSpec((tm, d), rows),
                  _resident((1, SSD_INNER)),
                  _resident((1, SSD_INNER)),
                  _resident((RET_V, d), layer),
                  _resident((SSD_INNER, d), layer),
                  _resident((d, d), layer)],
        out_specs=pl.BlockSpec((tm, d), rows),
        out_shape=jax.ShapeDtypeStruct((t, d), F32),
        compiler_params=_params("parallel"),
        name="out_proj_merge",
    )(yr, y_ssd, xc, gz, gates, gates, h,
      jnp.repeat(d_skip, SSD_HEADDIM).reshape(1, SSD_INNER), norm_w.reshape(1, SSD_INNER), w_ret, w_ssd, w_o)


def _ffn_kernel(h_ref, nw_ref, wgu_ref, wd_ref, nw2_ref, o_ref, on_ref):
    x = h_ref[...]
    xn = _rms(x, nw_ref[...]).astype(BF16)
    gate = jnp.dot(xn, wgu_ref[:, :D_FF], preferred_element_type=F32)
    up = jnp.dot(xn, wgu_ref[:, D_FF:], preferred_element_type=F32)
    out = x + jnp.dot((_silu(gate) * up).astype(BF16), wd_ref[...], preferred_element_type=F32)
    o_ref[...] = out
    on_ref[...] = _rms(out, nw2_ref[...]).astype(on_ref.dtype)


def _ffn(h, norm_w, w_gate_up, w_down, layer, next_norm_w, next_dtype, tm):
    t, d = h.shape
    rows = lambda i: (i, 0)
    return pl.pallas_call(
        _ffn_kernel,
        grid=(t // tm,),
        in_specs=[pl.BlockSpec((tm, d), rows),
                  _resident((1, d)),
                  _resident((d, 2 * D_FF), layer),
                  _resident((D_FF, d), layer),
                  _resident((1, d))],
        out_specs=[pl.BlockSpec((tm, d), rows), pl.BlockSpec((tm, d), rows)],
        out_shape=[jax.ShapeDtypeStruct((t, d), F32), jax.ShapeDtypeStruct((t, d), next_dtype)],
        compiler_params=_params("parallel"),
        name="swiglu_ffn",
    )(h, norm_w.reshape(1, d), w_gate_up, w_down, next_norm_w.reshape(1, d))


def kernel(x, norm_mix_w, w_in, ret_out, conv_w, conv_b, dt_bias, a_log, d_skip, ssd_norm_w, ssd_out, w_o,
           norm_ffn_w, w_gate_up, w_down, final_norm_w):
    batch, seq, d = x.shape
    depth = w_in.shape[0]
    assert d == D_MODEL and w_in.shape[2] == D_IN
    assert seq % RET_BLOCK == 0 and seq % SSD_BLOCK == 0
    t = batch * seq
    tm = min(PROJ_TM, seq)

    inv = (ROPE_BASE ** (-jnp.arange(ROPE_HALF, dtype=F32) / ROPE_HALF)).reshape(1, ROPE_HALF)
    cos, sin = _rope_table(inv, seq, tm)
    dmask, qdec, kdec, cdec = _retention_tables()

    w = w_in.astype(BF16)
    w_xbc = w[:, :, OFF_XBC:OFF_DT]
    w_gates_dt = jnp.concatenate(
        [w[:, :, OFF_GATES:], jnp.pad(w[:, :, OFF_DT:OFF_GATES], ((0, 0), (0, 0), (0, LANES - SSD_HEADS)))], axis=2)
    w_ret, w_ssd, w_out = ret_out.astype(BF16), ssd_out.astype(BF16), w_o.astype(BF16)
    w_gu, w_dn = w_gate_up.astype(BF16), w_down.astype(BF16)

    h = x.reshape(t, d)
    xn = _rmsnorm(h, norm_mix_w[0], BF16, tm)
    out = None
    for l in range(depth):
        qk = _qk_proj(xn, w, l, cos, sin, seq, tm)
        v, xc = _vxbc_proj(xn, w, w_xbc, l, conv_w, conv_b, seq, tm, VXBC_SPLIT)
        gz = _proj(xn, w, l, OFF_G, RET_V + SSD_INNER, "silu", BF16, tm, GZ_TN, "gz_proj")
        gates, dt = _gates_dt_proj(xn, w_gates_dt, l, tm)

        yr, y_ssd = _scans(qk, v, gz, dmask, qdec, kdec, cdec, xc, dt, dt_bias[l], a_log[l], batch, seq)
        h = _merge(yr, y_ssd, xc, gz, gates, h, d_skip[l], ssd_norm_w[l], w_ret, w_ssd, w_out, l,
                   min(MERGE_TM, seq))

        last = l == depth - 1
        next_w = final_norm_w if last else norm_mix_w[l + 1]
        h, nxt = _ffn(h, norm_ffn_w[l], w_gu, w_dn, l, next_w, F32 if last else BF16, min(FFN_TM, seq))
        if last:
            out = nxt
        else:
            xn = nxt
    return out.reshape(batch, seq, d)
```
